```python
import math
import jax, jax.numpy as jnp
from jax import lax
import numpy as np

D_MODEL = 2048
BATCH = 4
SEQ = 4096
DEPTH = 4

HEAD_DIM = 128
N_HEADS_A = 8
N_HEADS_B = 8
WIDTH_A = N_HEADS_A * HEAD_DIM
WIDTH_B = N_HEADS_B * HEAD_DIM
MIX_WIDTH = WIDTH_A + WIDTH_B
IN_WIDTH = 3 * MIX_WIDTH
DILATED_PATTERNS = ((128, 1), (512, 4), (2048, 16))
ROPE_THETA = 10000.0
SB_BLOCK = 128
PEER_HEADS = 8
PEER_N_KEYS = 128
PEER_N_EXPERTS = PEER_N_KEYS * PEER_N_KEYS
PEER_KEY_DIM = 256
PEER_KEY_HALF = PEER_KEY_DIM // 2
PEER_TOPK = 16
PEER_CHUNK = 128
PLE_DIM = 256
NORM_EPS = 1e-6
NEG_INF = -1e30

kernel_name = 'hybrid_dilated_stickbreaking_peer_block'

F32 = jnp.float32


def rms_norm(x, g):
    xf = x.astype(F32)
    y = xf * lax.rsqrt(jnp.mean(xf * xf, axis=-1, keepdims=True) + NORM_EPS)
    return (y * g.astype(F32)).astype(x.dtype)


def apply_rope(x, positions):
    half = HEAD_DIM // 2
    inv_freq = ROPE_THETA ** (-jnp.arange(half, dtype=F32) / half)
    ang = positions.astype(F32)[..., None] * inv_freq
    cos = jnp.cos(ang)[:, :, None, :]
    sin = jnp.sin(ang)[:, :, None, :]
    xf = x.astype(F32)
    x1, x2 = xf[..., :half], xf[..., half:]
    out = jnp.concatenate([x1 * cos - x2 * sin, x2 * cos + x1 * sin], axis=-1)
    return out.astype(x.dtype)


def causal_band_attention(q, k, v, w):
    n, L, h, hd = q.shape
    nb = -(-L // w)
    pad = nb * w - L
    def blocks(a):
        a = jnp.pad(a.astype(F32), ((0, 0), (0, pad), (0, 0), (0, 0)))
        return a.reshape(n, nb, w, h, hd)
    qb, kb, vb = blocks(q), blocks(k), blocks(v)
    def with_prev(a):
        prev = jnp.concatenate([jnp.zeros_like(a[:, :1]), a[:, :-1]], axis=1)
        return jnp.concatenate([prev, a], axis=2)
    kk, vv = with_prev(kb), with_prev(vb)
    s = jnp.einsum('nbqhd,nbkhd->nbhqk', qb, kk) * (hd ** -0.5)
    qi = jnp.arange(w)[:, None]
    kj = jnp.arange(2 * w)[None, :]
    rel = qi + w - kj
    band = (rel >= 0) & (rel <= w)
    key_ok = (jnp.arange(nb)[:, None, None] > 0) | (kj[None] >= w)
    mask = band[None] & key_ok
    s = jnp.where(mask[None, :, None], s, NEG_INF)
    m = jnp.max(s, axis=-1, keepdims=True)
    pe = jnp.exp(s - m)
    l = jnp.sum(pe, axis=-1, keepdims=True)
    o = jnp.einsum('nbhqk,nbkhd->nbqhd', pe / l, vv)
    lse = jnp.swapaxes((m + jnp.log(l))[..., 0], 2, 3)
    o = o.reshape(n, nb * w, h, hd)[:, :L]
    lse = lse.reshape(n, nb * w, h)[:, :L]
    return o, lse


def dilated_attention(q, k, v):
    b, s, h, hd = q.shape
    outs, lses = [], []
    for window, d in DILATED_PATTERNS:
        L = s // d
        def to_strided(a):
            return a.reshape(b, L, d, h, hd).transpose(0, 2, 1, 3, 4).reshape(b * d, L, h, hd)
        o, lse = causal_band_attention(to_strided(q), to_strided(k), to_strided(v), window // d)
        outs.append(o.reshape(b, d, L, h, hd).transpose(0, 2, 1, 3, 4).reshape(b, s, h, hd))
        lses.append(lse.reshape(b, d, L, h).transpose(0, 2, 1, 3).reshape(b, s, h))
    wts = jax.nn.softmax(jnp.stack(lses, axis=0), axis=0)
    o = jnp.sum(wts[..., None] * jnp.stack(outs, axis=0), axis=0)
    return o.astype(q.dtype)


def stick_breaking_attention(q, k, v):
    b, s, h, hd = q.shape
    nb = s // SB_BLOCK
    scale = hd ** -0.5
    kf, vf = k.astype(F32), v.astype(F32)
    qblocks = q.reshape(b, nb, SB_BLOCK, h, hd).transpose(1, 0, 2, 3, 4)
    key_pos = jnp.arange(s)
    def block(args):
        i, qb = args
        z = jnp.einsum('bqhd,bkhd->bhqk', qb.astype(F32), kf) * scale
        t = i * SB_BLOCK + jnp.arange(SB_BLOCK)
        mask = key_pos[None, :] < t[:, None]
        log_1m = jnp.where(mask, jax.nn.log_sigmoid(-z), 0.0)
        later = lax.cumsum(log_1m, axis=3, reverse=True) - log_1m
        a = jnp.where(mask, jnp.exp(jax.nn.log_sigmoid(z) + later), 0.0)
        return jnp.einsum('bhqk,bkhd->bqhd', a, vf)
    o = lax.map(block, (jnp.arange(nb), qblocks))
    return o.transpose(1, 0, 2, 3, 4).reshape(b, s, h, hd).astype(q.dtype)


def peer_ffn(x, w_q, sub_keys, u_tab, v_tab):
    b, s, d = x.shape
    t = b * s
    xt = x.reshape(t, d)
    qr = (xt @ w_q).reshape(t, PEER_HEADS, 2, PEER_KEY_HALF).astype(F32)
    sc = jnp.einsum('thcd,hcnd->thcn', qr, sub_keys.astype(F32))
    top_s, top_i = lax.top_k(sc, PEER_TOPK)
    cand_s = top_s[:, :, 0, :, None] + top_s[:, :, 1, None, :]
    cand_i = top_i[:, :, 0, :, None] * PEER_N_KEYS + top_i[:, :, 1, None, :]
    best_s, best_j = lax.top_k(cand_s.reshape(t, PEER_HEADS, PEER_TOPK * PEER_TOPK), PEER_TOPK)
    expert = jnp.take_along_axis(cand_i.reshape(t, PEER_HEADS, PEER_TOPK * PEER_TOPK), best_j, axis=-1)
    gates = jax.nn.softmax(best_s, axis=-1)
    nc = t // PEER_CHUNK
    def chunk(args):
        xc, ec, gc = args
        u = u_tab[ec]
        hid = jax.nn.gelu(jnp.einsum('cd,chkd->chk', xc, u).astype(F32), approximate=False)
        a = (hid * gc).astype(v_tab.dtype)
        return jnp.einsum('chk,chkd->cd', a, v_tab[ec])
    out = lax.map(chunk, (xt.reshape(nc, PEER_CHUNK, d),
                          expert.reshape(nc, PEER_CHUNK, PEER_HEADS, PEER_TOPK),
                          gates.reshape(nc, PEER_CHUNK, PEER_HEADS, PEER_TOPK)))
    return out.reshape(b, s, d).astype(x.dtype)


def setup_inputs(seed: int = 0) -> dict:
    key = jax.random.key(seed)
    ks = jax.random.split(key, 20)
    def nrm(k, shape, scale):
        return jax.random.normal(k, shape, F32) * scale
    def gain(k, shape):
        return 1.0 + 0.02 * jax.random.normal(k, shape, F32)
    x = jax.random.normal(ks[0], (BATCH, SEQ, D_MODEL), F32)
    p = jax.random.normal(ks[1], (DEPTH, BATCH, SEQ, PLE_DIM), F32)
    is_start = jax.random.bernoulli(ks[2], 1.0 / 1024, (BATCH, SEQ)).at[:, 0].set(True)
    idx = jnp.broadcast_to(jnp.arange(SEQ, dtype=jnp.int32), (BATCH, SEQ))
    doc_start = lax.cummax(jnp.where(is_start, idx, 0), axis=1)
    positions = (idx - doc_start).astype(jnp.int32)
    return {
        'x': x,
        'p': p,
        'positions': positions,
        'attn_norm': gain(ks[3], (DEPTH, D_MODEL)),
        'w_in': nrm(ks[4], (DEPTH, D_MODEL, IN_WIDTH), D_MODEL ** -0.5),
        'out_norm_a': gain(ks[5], (DEPTH, WIDTH_A)),
        'out_norm_b': gain(ks[6], (DEPTH, WIDTH_B)),
        'w_out': nrm(ks[7], (DEPTH, MIX_WIDTH, D_MODEL), MIX_WIDTH ** -0.5),
        'ffn_norm': gain(ks[8], (DEPTH, D_MODEL)),
        'peer_wq': nrm(ks[9], (DEPTH, D_MODEL, PEER_HEADS * PEER_KEY_DIM), D_MODEL ** -0.5),
        'peer_subkeys': nrm(ks[10], (DEPTH, PEER_HEADS, 2, PEER_N_KEYS, PEER_KEY_HALF), PEER_KEY_HALF ** -0.5),
        'peer_u': nrm(ks[11], (DEPTH, PEER_N_EXPERTS, D_MODEL), D_MODEL ** -0.5),
        'peer_v': nrm(ks[12], (DEPTH, PEER_N_EXPERTS, D_MODEL), PEER_HEADS ** -0.5),
        'ple_norm': gain(ks[13], (DEPTH, D_MODEL)),
        'ple_gate': nrm(ks[14], (DEPTH, D_MODEL, D_MODEL), D_MODEL ** -0.5),
        'ple_proj': nrm(ks[15], (DEPTH, PLE_DIM, D_MODEL), PLE_DIM ** -0.5),
        'final_norm': gain(ks[16], (D_MODEL,)),
    }


def reference(x, p, positions, attn_norm, w_in, out_norm_a, out_norm_b, w_out, ffn_norm,
              peer_wq, peer_subkeys, peer_u, peer_v, ple_norm, ple_gate, ple_proj, final_norm):
    b, s, _ = x.shape
    splits = (WIDTH_A, 2 * WIDTH_A, 3 * WIDTH_A, 3 * WIDTH_A + WIDTH_B, 3 * WIDTH_A + 2 * WIDTH_B)
    h = x
    for i in range(DEPTH):
        hn = rms_norm(h, attn_norm[i])
        proj = hn @ w_in[i]
        qa, ka, va, qb, kb, vb = jnp.split(proj, splits, axis=-1)
        qa = apply_rope(qa.reshape(b, s, N_HEADS_A, HEAD_DIM), positions)
        ka = apply_rope(ka.reshape(b, s, N_HEADS_A, HEAD_DIM), positions)
        va = va.reshape(b, s, N_HEADS_A, HEAD_DIM)
        oa = dilated_attention(qa, ka, va).reshape(b, s, WIDTH_A)
        ob = stick_breaking_attention(qb.reshape(b, s, N_HEADS_B, HEAD_DIM),
                                      kb.reshape(b, s, N_HEADS_B, HEAD_DIM),
                                      vb.reshape(b, s, N_HEADS_B, HEAD_DIM)).reshape(b, s, WIDTH_B)
        mixed = jnp.concatenate([rms_norm(oa, out_norm_a[i]), rms_norm(ob, out_norm_b[i])], axis=-1)
        h = h + (mixed @ w_out[i]).astype(h.dtype)
        h = h + peer_ffn(rms_norm(h, ffn_norm[i]), peer_wq[i], peer_subkeys[i], peer_u[i], peer_v[i])
        gate = jax.nn.sigmoid((rms_norm(h, ple_norm[i]) @ ple_gate[i]).astype(F32))
        h = h + ((p[i] @ ple_proj[i]).astype(F32) * gate).astype(h.dtype)
    return rms_norm(h, final_norm)
```

```python
import functools
import math

import jax
import jax.numpy as jnp
from jax import lax
from jax.experimental import pallas as pl
from jax.experimental.pallas import tpu as pltpu

F32 = jnp.float32
BF16 = jnp.bfloat16

HEAD_DIM = 128
N_HEADS_A = 8
N_HEADS_B = 8
WIDTH_A = N_HEADS_A * HEAD_DIM
WIDTH_B = N_HEADS_B * HEAD_DIM
DILATED_PATTERNS = ((128, 1), (512, 4), (2048, 16))
ROPE_THETA = 10000.0
PEER_HEADS = 8
PEER_N_KEYS = 128
PEER_TOPK = 16
NORM_EPS = 1e-6
NEG_INF = -1e30

LANES = 128
SUBLANES = 8
VMEM_LIMIT = 56 * 1024 * 1024
F32_EXP_ZERO = -104.0


def _cparams(*sem):
    return pltpu.CompilerParams(dimension_semantics=sem, vmem_limit_bytes=VMEM_LIMIT)


def _dot(a, b):
    return jnp.dot(a, b, preferred_element_type=F32)


def _dot_nt(a, b):
    return lax.dot_general(a, b, (((1,), (1,)), ((), ())), preferred_element_type=F32)


def _rms(x, g):
    return x * lax.rsqrt(jnp.mean(x * x, axis=-1, keepdims=True) + NORM_EPS) * g


def _rope_kernel(pos_ref, freq_ref, sign_ref, cos_ref, sin_ref):
    ang = pos_ref[...].astype(F32) * freq_ref[...]
    cos_ref[...] = jnp.cos(ang)
    sin_ref[...] = jnp.sin(ang) * sign_ref[...]


def _rope_tables(positions):
    t = positions.size
    half = HEAD_DIM // 2
    inv_freq = ROPE_THETA ** (-jnp.arange(half, dtype=F32) / half)
    freq = jnp.concatenate([inv_freq, inv_freq]).reshape(1, HEAD_DIM)
    sign = jnp.concatenate([-jnp.ones((half,), F32), jnp.ones((half,), F32)]).reshape(1, HEAD_DIM)
    tm = min(t, 1024)
    row = pl.BlockSpec((tm, HEAD_DIM), lambda i: (i, 0))
    one = pl.BlockSpec((1, HEAD_DIM), lambda i: (0, 0))
    return pl.pallas_call(
        _rope_kernel,
        grid=(t // tm,),
        in_specs=[pl.BlockSpec((tm, 1), lambda i: (i, 0)), one, one],
        out_specs=[row, row],
        out_shape=[jax.ShapeDtypeStruct((t, HEAD_DIM), F32)] * 2,
        compiler_params=_cparams("parallel"),
        name="rope_tables",
    )(positions.reshape(t, 1), freq, sign)


def _in_proj_kernel(x_ref, g_ref, w_ref, cos_ref, sin_ref, o_ref, xn_ref, *, scale):
    j = pl.program_id(1)

    @pl.when(j == 0)
    def _():
        xn_ref[...] = _rms(x_ref[...], g_ref[...]).astype(BF16)

    acc = _dot(xn_ref[...], w_ref[...])

    @pl.when(j < 2)
    def _():
        sc = jnp.where(j == 0, scale, 1.0).astype(F32)
        c = cos_ref[...] * sc
        s = sin_ref[...] * sc
        for hh in range(N_HEADS_A):
            cs = slice(hh * HEAD_DIM, (hh + 1) * HEAD_DIM)
            a = acc[:, cs]
            o_ref[:, cs] = (a * c + pltpu.roll(a, HEAD_DIM // 2, 1) * s).astype(BF16)

    @pl.when(j >= 2)
    def _():
        sc = jnp.where(j == 3, scale, 1.0).astype(F32)
        o_ref[...] = (acc * sc).astype(BF16)


def _in_proj(h2, g, w_bf, cos, sin):
    t, d = h2.shape
    n = w_bf.shape[1]
    tm = min(t, 512)
    tn = WIDTH_A
    return pl.pallas_call(
        functools.partial(_in_proj_kernel, scale=HEAD_DIM ** -0.5),
        grid=(t // tm, n // tn),
        in_specs=[
            pl.BlockSpec((tm, d), lambda i, j: (i, 0)),
            pl.BlockSpec((1, d), lambda i, j: (0, 0)),
            pl.BlockSpec((d, tn), lambda i, j: (0, j)),
            pl.BlockSpec((tm, HEAD_DIM), lambda i, j: (i, 0)),
            pl.BlockSpec((tm, HEAD_DIM), lambda i, j: (i, 0)),
        ],
        out_specs=pl.BlockSpec((tm, tn), lambda i, j: (i, j)),
        out_shape=jax.ShapeDtypeStruct((t, n), BF16),
        scratch_shapes=[pltpu.VMEM((tm, d), BF16)],
        compiler_params=_cparams("parallel", "arbitrary"),
        name="in_proj",
    )(h2, g.reshape(1, d), w_bf, cos, sin)


def _band_attn_kernel(q_ref, k_ref, v_ref, o_ref, lse_ref, *, hg, nblk):
    w = HEAD_DIM
    qi = lax.broadcasted_iota(jnp.int32, (w, w), 0)
    kj = lax.broadcasted_iota(jnp.int32, (w, w), 1)
    mask_cur = kj <= qi
    mask_prev = kj >= qi

    def body(blk, carry):
        r0 = pl.multiple_of(blk * w, w)
        p0 = pl.multiple_of(jnp.maximum(blk - 1, 0) * w, w)
        mp = jnp.logical_and(mask_prev, blk > 0)
        for hh in range(hg):
            cs = slice(hh * HEAD_DIM, (hh + 1) * HEAD_DIM)
            q = q_ref[pl.ds(r0, w), cs]
            s_c = jnp.where(mask_cur, _dot_nt(q, k_ref[pl.ds(r0, w), cs]), NEG_INF)
            s_p = jnp.where(mp, _dot_nt(q, k_ref[pl.ds(p0, w), cs]), NEG_INF)
            m = jnp.maximum(jnp.max(s_c, axis=-1, keepdims=True), jnp.max(s_p, axis=-1, keepdims=True))
            e_c = jnp.exp(s_c - m)
            e_p = jnp.exp(s_p - m)
            l = jnp.sum(e_c, axis=-1, keepdims=True) + jnp.sum(e_p, axis=-1, keepdims=True)
            o = _dot(e_c.astype(BF16), v_ref[pl.ds(r0, w), cs]) + _dot(e_p.astype(BF16), v_ref[pl.ds(p0, w), cs])
            o_ref[pl.ds(r0, w), cs] = (o / l).astype(BF16)
            lse_ref[pl.ds(r0, w), cs] = jnp.broadcast_to(m + jnp.log(l), (w, HEAD_DIM))
        return carry

    lax.fori_loop(0, nblk, body, 0)


def _band_attn(proj, b, s, dil, hg):
    n_in = proj.shape[1]
    L = s // dil
    view = proj.reshape(b, L, dil * n_in)
    cw = hg * HEAD_DIM
    per_res = n_in // cw
    per_tensor = WIDTH_A // cw
    ngrp = N_HEADS_A // hg

    def in_map(x):
        return lambda bi, r, g: (bi, 0, r * per_res + x * per_tensor + g)

    out_map = lambda bi, r, g: (bi, 0, r * ngrp + g)
    o, lse = pl.pallas_call(
        functools.partial(_band_attn_kernel, hg=hg, nblk=L // HEAD_DIM),
        grid=(b, dil, ngrp),
        in_specs=[pl.BlockSpec((None, L, cw), in_map(0)),
                  pl.BlockSpec((None, L, cw), in_map(1)),
                  pl.BlockSpec((None, L, cw), in_map(2))],
        out_specs=[pl.BlockSpec((None, L, cw), out_map), pl.BlockSpec((None, L, cw), out_map)],
        out_shape=[jax.ShapeDtypeStruct((b, L, dil * WIDTH_A), BF16),
                   jax.ShapeDtypeStruct((b, L, dil * WIDTH_A), F32)],
        compiler_params=_cparams("parallel", "parallel", "parallel"),
        name=f"band_attn_d{dil}",
    )(view, view, view)
    return o.reshape(b * s, WIDTH_A), lse.reshape(b * s, WIDTH_A)


def _sb_kernel(q_ref, k_ref, v_ref, o_ref, *, blk, nq):
    row = lax.broadcasted_iota(jnp.int32, (blk, blk), 0)
    col = lax.broadcasted_iota(jnp.int32, (blk, blk), 1)
    tri = jnp.where(row > col, 1.0, 0.0).astype(BF16)

    def qblock(i, carry):
        q0 = pl.multiple_of(i * blk, blk)
        q = q_ref[pl.ds(q0, blk), :]

        def cond(st):
            kb, c, _ = st
            return jnp.logical_and(kb >= 0, jnp.max(c) > F32_EXP_ZERO)

        def body(st):
            kb, c, acc = st
            k0 = pl.multiple_of(kb * blk, blk)
            z = _dot_nt(q, k_ref[pl.ds(k0, blk), :])
            valid = jnp.logical_or(kb < i, col < row)
            sp = jnp.maximum(z, 0.0) + jnp.log(1.0 + jnp.exp(-jnp.abs(z)))
            lm = jnp.where(valid, -sp, 0.0)
            hi = lm.astype(BF16)
            lo = (lm - hi.astype(F32)).astype(BF16)
            later = _dot(hi, tri) + _dot(lo, tri)
            a = jnp.where(valid, jnp.exp(z - sp + later + c), 0.0)
            acc = acc + _dot(a.astype(BF16), v_ref[pl.ds(k0, blk), :])
            c = c + jnp.sum(lm, axis=-1, keepdims=True)
            return kb - 1, c, acc

        _, _, acc = lax.while_loop(
            cond, body, (i, jnp.zeros((blk, 1), F32), jnp.zeros((blk, HEAD_DIM), F32)))
        o_ref[pl.ds(q0, blk), :] = acc.astype(BF16)
        return carry

    lax.fori_loop(0, nq, qblock, 0)


def _sb_attn(proj, b, s):
    n_in = proj.shape[1]
    view = proj.reshape(b, s, n_in)
    base = 3 * N_HEADS_A
    blk = min(s, 256)

    def in_map(x):
        return lambda bi, h: (bi, 0, base + x * N_HEADS_B + h)

    o = pl.pallas_call(
        functools.partial(_sb_kernel, blk=blk, nq=s // blk),
        grid=(b, N_HEADS_B),
        in_specs=[pl.BlockSpec((None, s, HEAD_DIM), in_map(0)),
                  pl.BlockSpec((None, s, HEAD_DIM), in_map(1)),
                  pl.BlockSpec((None, s, HEAD_DIM), in_map(2))],
        out_specs=pl.BlockSpec((None, s, HEAD_DIM), lambda bi, h: (bi, 0, h)),
        out_shape=jax.ShapeDtypeStruct((b, s, WIDTH_B), BF16),
        compiler_params=_cparams("parallel", "parallel"),
        name="stick_breaking",
    )(view, view, view)
    return o.reshape(b * s, WIDTH_B)


def _mix_out_kernel(o1, o2, o3, l1, l2, l3, ob_ref, ga_ref, gb_ref, w_ref, h_ref, out_ref):
    la, lb, lc = l1[...], l2[...], l3[...]
    m = jnp.maximum(jnp.maximum(la, lb), lc)
    wa, wb, wc = jnp.exp(la - m), jnp.exp(lb - m), jnp.exp(lc - m)
    oa = (wa * o1[...].astype(F32) + wb * o2[...].astype(F32) + wc * o3[...].astype(F32)) / (wa + wb + wc)
    na = _rms(oa, ga_ref[...]).astype(BF16)
    nb = _rms(ob_ref[...].astype(F32), gb_ref[...]).astype(BF16)
    y = _dot(na, w_ref[:WIDTH_A, :]) + _dot(nb, w_ref[WIDTH_A:, :])
    out_ref[...] = h_ref[...] + y


def _mix_out(o_list, lse_list, ob, ga, gb, w_bf, h2):
    t, d = h2.shape
    tm = min(t, 256)
    ra = pl.BlockSpec((tm, WIDTH_A), lambda i: (i, 0))
    rd = pl.BlockSpec((tm, d), lambda i: (i, 0))
    return pl.pallas_call(
        _mix_out_kernel,
        grid=(t // tm,),
        in_specs=[ra, ra, ra, ra, ra, ra, ra,
                  pl.BlockSpec((1, WIDTH_A), lambda i: (0, 0)),
                  pl.BlockSpec((1, WIDTH_B), lambda i: (0, 0)),
                  pl.BlockSpec(w_bf.shape, lambda i: (0, 0)),
                  rd],
        out_specs=rd,
        out_shape=jax.ShapeDtypeStruct((t, d), F32),
        compiler_params=_cparams("parallel"),
        name="mix_out_proj",
    )(*o_list, *lse_list, ob, ga.reshape(1, -1), gb.reshape(1, -1), w_bf, h2)


def _peer_scores_kernel(h_ref, g_ref, wq_ref, knh_ref, khn_ref, xn_ref, scnh_ref, s2_ref):
    xn = _rms(h_ref[...], g_ref[...]).astype(BF16)
    xn_ref[...] = xn
    qr = _dot(xn, wq_ref[...]).astype(BF16)
    scnh_ref[...] = _dot_nt(knh_ref[...], qr)
    s2_ref[...] = _dot_nt(khn_ref[...], qr)


def _peer_scores(h2, g, wq_bf, knh, khn):
    t, d = h2.shape
    tm = min(t, 512)
    full = lambda a: pl.BlockSpec(a.shape, lambda i: (0, 0))
    return pl.pallas_call(
        _peer_scores_kernel,
        grid=(t // tm,),
        in_specs=[pl.BlockSpec((tm, d), lambda i: (i, 0)),
                  pl.BlockSpec((1, d), lambda i: (0, 0)),
                  full(wq_bf), full(knh), full(khn)],
        out_specs=[pl.BlockSpec((tm, d), lambda i: (i, 0)),
                   pl.BlockSpec((knh.shape[0], tm), lambda i: (0, i)),
                   pl.BlockSpec((khn.shape[0], tm), lambda i: (0, i))],
        out_shape=[jax.ShapeDtypeStruct((t, d), BF16),
                   jax.ShapeDtypeStruct((knh.shape[0], t), F32),
                   jax.ShapeDtypeStruct((khn.shape[0], t), F32)],
        compiler_params=_cparams("parallel"),
        name="peer_scores",
    )(h2, g.reshape(1, d), wq_bf, knh, khn)


def _oddeven_merge_sort_pairs(n):
    pairs = []

    def merge(lo, m, r):
        step = r * 2
        if step < m:
            merge(lo, m, step)
            merge(lo + r, m, step)
            for i in range(lo + r, lo + m - r, step):
                pairs.append((i, i + r))
        else:
            pairs.append((lo, lo + r))

    def sort(lo, m):
        if m > 1:
            h = m // 2
            sort(lo, h)
            sort(lo + h, h)
            merge(lo, m, 1)

    sort(0, n)
    return pairs


_SORT16 = _oddeven_merge_sort_pairs(PEER_TOPK)


def _sort_desc(vals):
    vals = list(vals)
    for i, j in _SORT16:
        a, b = vals[i], vals[j]
        vals[i], vals[j] = jnp.maximum(a, b), jnp.minimum(a, b)
    return vals


def _merge_top(a, b):
    n = len(a)
    c = [jnp.maximum(a[i], b[n - 1 - i]) for i in range(n)]
    stride = n // 2
    while stride >= 1:
        for i in range(n):
            if (i // stride) % 2 == 0:
                x, y = c[i], c[i + stride]
                c[i], c[i + stride] = jnp.maximum(x, y), jnp.minimum(x, y)
        stride //= 2
    return c


def _top_sorted(vals):
    k = PEER_TOPK
    groups = [_sort_desc(vals[i:i + k]) for i in range(0, len(vals), k)]
    while len(groups) > 1:
        groups = [_merge_top(groups[i], groups[i + 1]) for i in range(0, len(groups), 2)]
    return groups[0]


def _peer_select_kernel(scnh_ref, s2_ref, c1_ref, g1_ref, e2_ref, *, nchunk):
    nk, nh, k = PEER_N_KEYS, PEER_HEADS, PEER_TOPK
    big = -NEG_INF
    for lc in range(nchunk):
        ls = slice(lc * LANES, (lc + 1) * LANES)
        x1 = [scnh_ref[n * nh:(n + 1) * nh, ls] for n in range(nk)]
        x2 = [scnh_ref[(nk + n) * nh:(nk + n + 1) * nh, ls] for n in range(nk)]
        a = _top_sorted(x1)
        b = _top_sorted(x2)
        cand = [a[i] + b[j] for i in range(k) for j in range(k) if (i + 1) * (j + 1) <= k]
        pad = [jnp.full_like(a[0], NEG_INF)] * (-len(cand) % k)
        top = _top_sorted(cand + pad)
        c16 = top[k - 1]
        c17 = functools.reduce(jnp.maximum, [jnp.where(c < c16, c, NEG_INF) for c in cand])
        thr = 0.5 * (c16 + c17)
        mx = a[0] + b[0]
        z = functools.reduce(lambda u, v: u + v, [jnp.where(c > thr, jnp.exp(c - mx), 0.0) for c in cand])
        inv_z = 1.0 / z
        for n in range(nk):
            keep = x1[n] >= a[k - 1]
            c1_ref[n * nh:(n + 1) * nh, ls] = jnp.where(keep, thr - x1[n], big)
            g1_ref[n * nh:(n + 1) * nh, ls] = jnp.exp(x1[n] - a[0])
        for h in range(nh):
            s2 = s2_ref[h * nk:(h + 1) * nk, ls]
            e = jnp.exp(s2 - b[0][h:h + 1, :]) * inv_z[h:h + 1, :]
            e2_ref[h * nk:(h + 1) * nk, ls] = jnp.where(s2 >= b[k - 1][h:h + 1, :], e, 0.0)


def _peer_select(scnh, s2):
    t = s2.shape[1]
    tm = min(t, 256)
    rows = s2.shape[0]
    spec = pl.BlockSpec((rows, tm), lambda i: (0, i))
    return pl.pallas_call(
        functools.partial(_peer_select_kernel, nchunk=tm // LANES),
        grid=(t // tm,),
        in_specs=[pl.BlockSpec((scnh.shape[0], tm), lambda i: (0, i)), spec],
        out_specs=[spec, spec, spec],
        out_shape=[jax.ShapeDtypeStruct((rows, t), F32)] * 3,
        compiler_params=_cparams("parallel"),
        name="peer_select",
    )(scnh, s2)


def _gelu(x):
    return 0.5 * x * (1.0 + lax.erf(x * (2.0 ** -0.5)))


def _peer_mix_kernel(xt_ref, u_ref, vt_ref, s2_ref, e2_ref, c1_ref, g1_ref, out_ref, hid_ref, a_ref, *, te, tm):
    e = pl.program_id(1)
    nk, nh = PEER_N_KEYS, PEER_HEADS
    hid_ref[...] = _dot(u_ref[...], xt_ref[...])

    def lane_chunk(lc, carry):
        l0 = pl.multiple_of(lc * LANES, LANES)
        for ii in range(te // nk):
            w = jnp.zeros((nk, LANES), F32)
            for h in range(nh):
                r = ii * nh + h
                c = c1_ref[r:r + 1, pl.ds(l0, LANES)]
                g = g1_ref[r:r + 1, pl.ds(l0, LANES)]
                s2 = s2_ref[h * nk:(h + 1) * nk, pl.ds(l0, LANES)]
                e2 = e2_ref[h * nk:(h + 1) * nk, pl.ds(l0, LANES)]
                w = w + jnp.where(s2 >= c, e2 * g, 0.0)
            hid = hid_ref[ii * nk:(ii + 1) * nk, pl.ds(l0, LANES)]
            a_ref[ii * nk:(ii + 1) * nk, pl.ds(l0, LANES)] = (_gelu(hid) * w).astype(BF16)
        return carry

    lax.fori_loop(0, tm // LANES, lane_chunk, 0)
    contrib = _dot(vt_ref[...], a_ref[...])

    @pl.when(e == 0)
    def _():
        out_ref[...] = contrib

    @pl.when(e > 0)
    def _():
        out_ref[...] += contrib


def _peer_mix(xt, u_bf, vt_bf, s2, e2, c1, g1):
    d, t = xt.shape
    n_exp = u_bf.shape[0]
    tm = min(t, 512)
    te = 512
    rows = s2.shape[0]
    crow = te // PEER_N_KEYS * PEER_HEADS
    tok = pl.BlockSpec((rows, tm), lambda i, e: (0, i))
    sel = pl.BlockSpec((crow, tm), lambda i, e: (e, i))
    return pl.pallas_call(
        functools.partial(_peer_mix_kernel, te=te, tm=tm),
        grid=(t // tm, n_exp // te),
        in_specs=[pl.BlockSpec((d, tm), lambda i, e: (0, i)),
                  pl.BlockSpec((te, d), lambda i, e: (e, 0)),
                  pl.BlockSpec((d, te), lambda i, e: (0, e)),
                  tok, tok, sel, sel],
        out_specs=pl.BlockSpec((d, tm), lambda i, e: (0, i)),
        out_shape=jax.ShapeDtypeStruct((d, t), F32),
        scratch_shapes=[pltpu.VMEM((te, tm), F32), pltpu.VMEM((te, tm), BF16)],
        compiler_params=_cparams("parallel", "arbitrary"),
        name="peer_mix",
    )(xt, u_bf, vt_bf, s2, e2, c1, g1)


def _ple_kernel(h_ref, po_ref, g_ref, wg_ref, p_ref, wp_ref, fn_ref, out_ref, *, final):
    h = h_ref[...] + po_ref[...]
    hn = _rms(h, g_ref[...]).astype(BF16)
    gate = jax.nn.sigmoid(_dot(hn, wg_ref[...]))
    emb = _dot(p_ref[...].astype(BF16), wp_ref[...])
    y = h + emb * gate
    if final:
        y = _rms(y, fn_ref[...])
    out_ref[...] = y


def _ple(h2, peer_out, g, wg_bf, p2, wp_bf, fn, final):
    t, d = h2.shape
    tm = min(t, 256)
    rd = pl.BlockSpec((tm, d), lambda i: (i, 0))
    one = pl.BlockSpec((1, d), lambda i: (0, 0))
    full = lambda a: pl.BlockSpec(a.shape, lambda i: (0, 0))
    return pl.pallas_call(
        functools.partial(_ple_kernel, final=final),
        grid=(t // tm,),
        in_specs=[rd, rd, one, full(wg_bf), pl.BlockSpec((tm, p2.shape[1]), lambda i: (i, 0)), full(wp_bf), one],
        out_specs=rd,
        out_shape=jax.ShapeDtypeStruct((t, d), F32),
        compiler_params=_cparams("parallel"),
        name="ple_final" if final else "ple",
    )(h2, peer_out, g.reshape(1, d), wg_bf, p2, wp_bf, fn.reshape(1, d))


def _subkey_matrices(sub):
    nh, _, nk, kd = sub.shape
    eye_h = jnp.eye(nh, dtype=sub.dtype)
    eye_c = jnp.eye(2, dtype=sub.dtype)
    knh = jnp.einsum("hcnd,hg,ce->cnhged", sub, eye_h, eye_c).reshape(2 * nk * nh, nh * 2 * kd)
    khn = jnp.einsum("hnd,hg,e->hnged", sub[:, 1], eye_h, eye_c[1]).reshape(nh * nk, nh * 2 * kd)
    return knh.astype(BF16), khn.astype(BF16)


def kernel(x, p, positions, attn_norm, w_in, out_norm_a, out_norm_b, w_out, ffn_norm, peer_wq, peer_subkeys,
           peer_u, peer_v, ple_norm, ple_gate, ple_proj, final_norm):
    b, s, d = x.shape
    depth = w_in.shape[0]
    t = b * s
    assert s % (LANES * max(dil for _, dil in DILATED_PATTERNS)) == 0
    assert all(win // dil == HEAD_DIM for win, dil in DILATED_PATTERNS)
    cos, sin = _rope_tables(positions)
    h = x.reshape(t, d)
    for i in range(depth):
        proj = _in_proj(h, attn_norm[i], w_in[i].astype(BF16), cos, sin)
        outs = [_band_attn(proj, b, s, dil, hg) for (_, dil), hg in zip(DILATED_PATTERNS, (2, 8, 8))]
        ob = _sb_attn(proj, b, s)
        h = _mix_out([o for o, _ in outs], [l for _, l in outs], ob, out_norm_a[i], out_norm_b[i],
                     w_out[i].astype(BF16), h)
        knh, khn = _subkey_matrices(peer_subkeys[i])
        xn, scnh, s2 = _peer_scores(h, ffn_norm[i], peer_wq[i].astype(BF16), knh, khn)
        c1, g1, e2 = _peer_select(scnh, s2)
        peer_t = _peer_mix(xn.T, peer_u[i].astype(BF16), peer_v[i].T.astype(BF16), s2, e2, c1, g1)
        h = _ple(h, peer_t.T, ple_norm[i], ple_gate[i].astype(BF16), p[i].reshape(t, -1),
                 ple_proj[i].astype(BF16), final_norm, final=(i == depth - 1))
    return h.reshape(b, s, d)
```

```python
import functools
import math

import jax
import jax.numpy as jnp
from jax import lax
from jax.experimental import pallas as pl
from jax.experimental.pallas import tpu as pltpu

F32 = jnp.float32
BF16 = jnp.bfloat16

HEAD_DIM = 128
N_HEADS_A = 8
N_HEADS_B = 8
WIDTH_A = N_HEADS_A * HEAD_DIM
WIDTH_B = N_HEADS_B * HEAD_DIM
DILATED_PATTERNS = ((128, 1), (512, 4), (2048, 16))
ROPE_THETA = 10000.0
PEER_HEADS = 8
PEER_N_KEYS = 128
PEER_TOPK = 16
NORM_EPS = 1e-6
NEG_INF = -1e30

LANES = 128
SUBLANES = 8
VMEM_LIMIT = 56 * 1024 * 1024
F32_EXP_ZERO = -104.0


def _cparams(*sem):
    return pltpu.CompilerParams(dimension_semantics=sem, vmem_limit_bytes=VMEM_LIMIT)


def _dot(a, b):
    return jnp.dot(a, b, preferred_element_type=F32)


def _dot_nt(a, b):
    return lax.dot_general(a, b, (((1,), (1,)), ((), ())), preferred_element_type=F32)


def _rms(x, g):
    return x * lax.rsqrt(jnp.mean(x * x, axis=-1, keepdims=True) + NORM_EPS) * g


def _rope_kernel(pos_ref, freq_ref, sign_ref, cos_ref, sin_ref):
    ang = pos_ref[...].astype(F32) * freq_ref[...]
    cos_ref[...] = jnp.cos(ang)
    sin_ref[...] = jnp.sin(ang) * sign_ref[...]


def _rope_tables(positions):
    t = positions.size
    half = HEAD_DIM // 2
    inv_freq = ROPE_THETA ** (-jnp.arange(half, dtype=F32) / half)
    freq = jnp.concatenate([inv_freq, inv_freq]).reshape(1, HEAD_DIM)
    sign = jnp.concatenate([-jnp.ones((half,), F32), jnp.ones((half,), F32)]).reshape(1, HEAD_DIM)
    tm = min(t, 1024)
    row = pl.BlockSpec((tm, HEAD_DIM), lambda i: (i, 0))
    one = pl.BlockSpec((1, HEAD_DIM), lambda i: (0, 0))
    return pl.pallas_call(
        _rope_kernel,
        grid=(t // tm,),
        in_specs=[pl.BlockSpec((tm, 1), lambda i: (i, 0)), one, one],
        out_specs=[row, row],
        out_shape=[jax.ShapeDtypeStruct((t, HEAD_DIM), F32)] * 2,
        compiler_params=_cparams("parallel"),
        name="rope_tables",
    )(positions.reshape(t, 1), freq, sign)


def _in_proj_kernel(x_ref, g_ref, w_ref, cos_ref, sin_ref, o_ref, xn_ref, *, scale):
    j = pl.program_id(1)

    @pl.when(j == 0)
    def _():
        xn_ref[...] = _rms(x_ref[...], g_ref[...]).astype(BF16)

    acc = _dot(xn_ref[...], w_ref[...])

    @pl.when(j < 2)
    def _():
        sc = jnp.where(j == 0, scale, 1.0).astype(F32)
        c = cos_ref[...] * sc
        s = sin_ref[...] * sc
        for hh in range(N_HEADS_A):
            cs = slice(hh * HEAD_DIM, (hh + 1) * HEAD_DIM)
            a = acc[:, cs]
            o_ref[:, cs] = (a * c + pltpu.roll(a, HEAD_DIM // 2, 1) * s).astype(BF16)

    @pl.when(j >= 2)
    def _():
        sc = jnp.where(j == 3, scale, 1.0).astype(F32)
        o_ref[...] = (acc * sc).astype(BF16)


def _in_proj(h2, g, w_bf, cos, sin):
    t, d = h2.shape
    n = w_bf.shape[1]
    tm = min(t, 512)
    tn = WIDTH_A
    return pl.pallas_call(
        functools.partial(_in_proj_kernel, scale=HEAD_DIM ** -0.5),
        grid=(t // tm, n // tn),
        in_specs=[
            pl.BlockSpec((tm, d), lambda i, j: (i, 0)),
            pl.BlockSpec((1, d), lambda i, j: (0, 0)),
            pl.BlockSpec((d, tn), lambda i, j: (0, j)),
            pl.BlockSpec((tm, HEAD_DIM), lambda i, j: (i, 0)),
            pl.BlockSpec((tm, HEAD_DIM), lambda i, j: (i, 0)),
        ],
        out_specs=pl.BlockSpec((tm, tn), lambda i, j: (i, j)),
        out_shape=jax.ShapeDtypeStruct((t, n), BF16),
        scratch_shapes=[pltpu.VMEM((tm, d), BF16)],
        compiler_params=_cparams("parallel", "arbitrary"),
        name="in_proj",
    )(h2, g.reshape(1, d), w_bf, cos, sin)


def _band_attn_kernel(q_ref, k_ref, v_ref, o_ref, lse_ref, *, hg, nblk):
    w = HEAD_DIM
    qi = lax.broadcasted_iota(jnp.int32, (w, w), 0)
    kj = lax.broadcasted_iota(jnp.int32, (w, w), 1)
    mask_cur = kj <= qi
    mask_prev = kj >= qi

    def body(blk, carry):
        r0 = pl.multiple_of(blk * w, w)
        p0 = pl.multiple_of(jnp.maximum(blk - 1, 0) * w, w)
        mp = jnp.logical_and(mask_prev, blk > 0)
        for hh in range(hg):
            cs = slice(hh * HEAD_DIM, (hh + 1) * HEAD_DIM)
            q = q_ref[pl.ds(r0, w), cs]
            s_c = jnp.where(mask_cur, _dot_nt(q, k_ref[pl.ds(r0, w), cs]), NEG_INF)
            s_p = jnp.where(mp, _dot_nt(q, k_ref[pl.ds(p0, w), cs]), NEG_INF)
            m = jnp.maximum(jnp.max(s_c, axis=-1, keepdims=True), jnp.max(s_p, axis=-1, keepdims=True))
            e_c = jnp.exp(s_c - m)
            e_p = jnp.exp(s_p - m)
            l = jnp.sum(e_c, axis=-1, keepdims=True) + jnp.sum(e_p, axis=-1, keepdims=True)
            o = _dot(e_c.astype(BF16), v_ref[pl.ds(r0, w), cs]) + _dot(e_p.astype(BF16), v_ref[pl.ds(p0, w), cs])
            o_ref[pl.ds(r0, w), cs] = (o / l).astype(BF16)
            lse_ref[pl.ds(r0, w), cs] = jnp.broadcast_to(m + jnp.log(l), (w, HEAD_DIM))
        return carry

    lax.fori_loop(0, nblk, body, 0)


def _band_attn(proj, b, s, dil, hg):
    n_in = proj.shape[1]
    L = s // dil
    view = proj.reshape(b, L, dil * n_in)
    cw = hg * HEAD_DIM
    per_res = n_in // cw
    per_tensor = WIDTH_A // cw
    ngrp = N_HEADS_A // hg

    def in_map(x):
        return lambda bi, r, g: (bi, 0, r * per_res + x * per_tensor + g)

    out_map = lambda bi, r, g: (bi, 0, r * ngrp + g)
    o, lse = pl.pallas_call(
        functools.partial(_band_attn_kernel, hg=hg, nblk=L // HEAD_DIM),
        grid=(b, dil, ngrp),
        in_specs=[pl.BlockSpec((None, L, cw), in_map(0)),
                  pl.BlockSpec((None, L, cw), in_map(1)),
                  pl.BlockSpec((None, L, cw), in_map(2))],
        out_specs=[pl.BlockSpec((None, L, cw), out_map), pl.BlockSpec((None, L, cw), out_map)],
        out_shape=[jax.ShapeDtypeStruct((b, L, dil * WIDTH_A), BF16),
                   jax.ShapeDtypeStruct((b, L, dil * WIDTH_A), F32)],
        compiler_params=_cparams("parallel", "parallel", "parallel"),
        name=f"band_attn_d{dil}",
    )(view, view, view)
    return o.reshape(b * s, WIDTH_A), lse.reshape(b * s, WIDTH_A)


def _sb_kernel(q_ref, k_ref, v_ref, o_ref, *, blk, nq):
    row = lax.broadcasted_iota(jnp.int32, (blk, blk), 0)
    col = lax.broadcasted_iota(jnp.int32, (blk, blk), 1)
    tri = jnp.where(row > col, 1.0, 0.0).astype(BF16)

    def qblock(i, carry):
        q0 = pl.multiple_of(i * blk, blk)
        q = q_ref[pl.ds(q0, blk), :]

        def cond(st):
            kb, c, _ = st
            return jnp.logical_and(kb >= 0, jnp.max(c) > F32_EXP_ZERO)

        def body(st):
            kb, c, acc = st
            k0 = pl.multiple_of(kb * blk, blk)
            z = _dot_nt(q, k_ref[pl.ds(k0, blk), :])
            valid = jnp.logical_or(kb < i, col < row)
            sp = jnp.maximum(z, 0.0) + jnp.log(1.0 + jnp.exp(-jnp.abs(z)))
            lm = jnp.where(valid, -sp, 0.0)
            hi = lm.astype(BF16)
            lo = (lm - hi.astype(F32)).astype(BF16)
            later = _dot(hi, tri) + _dot(lo, tri)
            a = jnp.where(valid, jnp.exp(z - sp + later + c), 0.0)
            acc = acc + _dot(a.astype(BF16), v_ref[pl.ds(k0, blk), :])
            c = c + jnp.sum(lm, axis=-1, keepdims=True)
            return kb - 1, c, acc

        _, _, acc = lax.while_loop(
            cond, body, (i, jnp.zeros((blk, 1), F32), jnp.zeros((blk, HEAD_DIM), F32)))
        o_ref[pl.ds(q0, blk), :] = acc.astype(BF16)
        return carry

    lax.fori_loop(0, nq, qblock, 0)


def _sb_attn(proj, b, s):
    n_in = proj.shape[1]
    view = proj.reshape(b, s, n_in)
    base = 3 * N_HEADS_A
    blk = min(s, 256)

    def in_map(x):
        return lambda bi, h: (bi, 0, base + x * N_HEADS_B + h)

    o = pl.pallas_call(
        functools.partial(_sb_kernel, blk=blk, nq=s // blk),
        grid=(b, N_HEADS_B),
        in_specs=[pl.BlockSpec((None, s, HEAD_DIM), in_map(0)),
                  pl.BlockSpec((None, s, HEAD_DIM), in_map(1)),
                  pl.BlockSpec((None, s, HEAD_DIM), in_map(2))],
        out_specs=pl.BlockSpec((None, s, HEAD_DIM), lambda bi, h: (bi, 0, h)),
        out_shape=jax.ShapeDtypeStruct((b, s, WIDTH_B), BF16),
        compiler_params=_cparams("parallel", "parallel"),
        name="stick_breaking",
    )(view, view, view)
    return o.reshape(b * s, WIDTH_B)


def _mix_out_kernel(o1, o2, o3, l1, l2, l3, ob_ref, ga_ref, gb_ref, w_ref, h_ref, out_ref):
    la, lb, lc = l1[...], l2[...], l3[...]
    m = jnp.maximum(jnp.maximum(la, lb), lc)
    wa, wb, wc = jnp.exp(la - m), jnp.exp(lb - m), jnp.exp(lc - m)
    oa = (wa * o1[...].astype(F32) + wb * o2[...].astype(F32) + wc * o3[...].astype(F32)) / (wa + wb + wc)
    na = _rms(oa, ga_ref[...]).astype(BF16)
    nb = _rms(ob_ref[...].astype(F32), gb_ref[...]).astype(BF16)
    y = _dot(na, w_ref[:WIDTH_A, :]) + _dot(nb, w_ref[WIDTH_A:, :])
    out_ref[...] = h_ref[...] + y


def _mix_out(o_list, lse_list, ob, ga, gb, w_bf, h2):
    t, d = h2.shape
    tm = min(t, 256)
    ra = pl.BlockSpec((tm, WIDTH_A), lambda i: (i, 0))
    rd = pl.BlockSpec((tm, d), lambda i: (i, 0))
    return pl.pallas_call(
        _mix_out_kernel,
        grid=(t // tm,),
        in_specs=[ra, ra, ra, ra, ra, ra, ra,
                  pl.BlockSpec((1, WIDTH_A), lambda i: (0, 0)),
                  pl.BlockSpec((1, WIDTH_B), lambda i: (0, 0)),
                  pl.BlockSpec(w_bf.shape, lambda i: (0, 0)),
                  rd],
        out_specs=rd,
        out_shape=jax.ShapeDtypeStruct((t, d), F32),
        compiler_params=_cparams("parallel"),
        name="mix_out_proj",
    )(*o_list, *lse_list, ob, ga.reshape(1, -1), gb.reshape(1, -1), w_bf, h2)


def _peer_scores_kernel(h_ref, g_ref, wq_ref, knh_ref, khn_ref, xn_ref, scnh_ref, s2_ref):
    xn = _rms(h_ref[...], g_ref[...]).astype(BF16)
    xn_ref[...] = xn
    qr = _dot(xn, wq_ref[...]).astype(BF16)
    scnh_ref[...] = _dot_nt(knh_ref[...], qr)
    s2 = _dot_nt(khn_ref[...], qr)
    for lt in range(s2_ref.shape[0]):
        s2_ref[lt] = s2[:, lt * LANES:(lt + 1) * LANES]


def _peer_scores(h2, g, wq_bf, knh, khn):
    t, d = h2.shape
    tm = min(t, 512)
    full = lambda a: pl.BlockSpec(a.shape, lambda i: (0, 0))
    return pl.pallas_call(
        _peer_scores_kernel,
        grid=(t // tm,),
        in_specs=[pl.BlockSpec((tm, d), lambda i: (i, 0)),
                  pl.BlockSpec((1, d), lambda i: (0, 0)),
                  full(wq_bf), full(knh), full(khn)],
        out_specs=[pl.BlockSpec((tm, d), lambda i: (i, 0)),
                   pl.BlockSpec((knh.shape[0], tm), lambda i: (0, i)),
                   pl.BlockSpec((tm // LANES, khn.shape[0], LANES), lambda i: (i, 0, 0))],
        out_shape=[jax.ShapeDtypeStruct((t, d), BF16),
                   jax.ShapeDtypeStruct((knh.shape[0], t), F32),
                   jax.ShapeDtypeStruct((t // LANES, khn.shape[0], LANES), F32)],
        compiler_params=_cparams("parallel"),
        name="peer_scores",
    )(h2, g.reshape(1, d), wq_bf, knh, khn)


def _oddeven_merge_sort_pairs(n):
    pairs = []

    def merge(lo, m, r):
        step = r * 2
        if step < m:
            merge(lo, m, step)
            merge(lo + r, m, step)
            for i in range(lo + r, lo + m - r, step):
                pairs.append((i, i + r))
        else:
            pairs.append((lo, lo + r))

    def sort(lo, m):
        if m > 1:
            h = m // 2
            sort(lo, h)
            sort(lo + h, h)
            merge(lo, m, 1)

    sort(0, n)
    return pairs


_SORT16 = _oddeven_merge_sort_pairs(PEER_TOPK)


def _sort_desc(vals):
    vals = list(vals)
    for i, j in _SORT16:
        a, b = vals[i], vals[j]
        vals[i], vals[j] = jnp.maximum(a, b), jnp.minimum(a, b)
    return vals


def _merge_top(a, b):
    n = len(a)
    c = [jnp.maximum(a[i], b[n - 1 - i]) for i in range(n)]
    stride = n // 2
    while stride >= 1:
        for i in range(n):
            if (i // stride) % 2 == 0:
                x, y = c[i], c[i + stride]
                c[i], c[i + stride] = jnp.maximum(x, y), jnp.minimum(x, y)
        stride //= 2
    return c


def _top_sorted(vals):
    k = PEER_TOPK
    groups = [_sort_desc(vals[i:i + k]) for i in range(0, len(vals), k)]
    while len(groups) > 1:
        groups = [_merge_top(groups[i], groups[i + 1]) for i in range(0, len(groups), 2)]
    return groups[0]


def _peer_select_kernel(scnh_ref, s2_ref, c1_ref, g1_ref, e2_ref, *, nchunk):
    nk, nh, k = PEER_N_KEYS, PEER_HEADS, PEER_TOPK
    big = -NEG_INF
    for lc in range(nchunk):
        ls = slice(lc * LANES, (lc + 1) * LANES)
        x1 = [scnh_ref[n * nh:(n + 1) * nh, ls] for n in range(nk)]
        x2 = [scnh_ref[(nk + n) * nh:(nk + n + 1) * nh, ls] for n in range(nk)]
        a = _top_sorted(x1)
        b = _top_sorted(x2)
        cand = [a[i] + b[j] for i in range(k) for j in range(k) if (i + 1) * (j + 1) <= k]
        pad = [jnp.full_like(a[0], NEG_INF)] * (-len(cand) % k)
        top = _top_sorted(cand + pad)
        c16 = top[k - 1]
        c17 = functools.reduce(jnp.maximum, [jnp.where(c < c16, c, NEG_INF) for c in cand])
        thr = 0.5 * (c16 + c17)
        mx = a[0] + b[0]
        z = functools.reduce(lambda u, v: u + v, [jnp.where(c > thr, jnp.exp(c - mx), 0.0) for c in cand])
        inv_z = 1.0 / z
        for n in range(nk):
            keep = x1[n] >= a[k - 1]
            c1_ref[lc, n * nh:(n + 1) * nh, :] = jnp.where(keep, thr - x1[n], big)
            g1_ref[lc, n * nh:(n + 1) * nh, :] = jnp.exp(x1[n] - a[0])
        for h in range(nh):
            s2 = s2_ref[lc, h * nk:(h + 1) * nk, :]
            e = jnp.exp(s2 - b[0][h:h + 1, :]) * inv_z[h:h + 1, :]
            e2_ref[lc, h * nk:(h + 1) * nk, :] = jnp.where(s2 >= b[k - 1][h:h + 1, :], e, 0.0)


def _peer_select(scnh, s2):
    nt, rows, _ = s2.shape
    nchunk = min(nt, 2)
    spec = pl.BlockSpec((nchunk, rows, LANES), lambda i: (i, 0, 0))
    return pl.pallas_call(
        functools.partial(_peer_select_kernel, nchunk=nchunk),
        grid=(nt // nchunk,),
        in_specs=[pl.BlockSpec((scnh.shape[0], nchunk * LANES), lambda i: (0, i)), spec],
        out_specs=[spec, spec, spec],
        out_shape=[jax.ShapeDtypeStruct(s2.shape, F32)] * 3,
        compiler_params=_cparams("parallel"),
        name="peer_select",
    )(scnh, s2)


def _gelu(x):
    return 0.5 * x * (1.0 + lax.erf(x * (2.0 ** -0.5)))


def _peer_mix_kernel(xt_ref, u_ref, vt_ref, s2_ref, e2_ref, c1_ref, g1_ref, out_ref, hid_ref, a_ref, *, te, tm, cw):
    nk, nh = PEER_N_KEYS, PEER_HEADS
    nslab = te // nk
    jrows = nk // 2
    tiles = cw // LANES

    @pl.when(pl.program_id(1) == 0)
    def _():
        out_ref[...] = jnp.zeros_like(out_ref)

    def hidden(c):
        hid = _dot(u_ref[...], xt_ref[:, c * cw:(c + 1) * cw])
        for l in range(tiles):
            hid_ref[c * tiles + l] = hid[:, l * LANES:(l + 1) * LANES]

    def gate(c):
        for l in range(tiles):
            lt = c * tiles + l
            for jh in range(2):
                w = [jnp.zeros((jrows, LANES), F32) for _ in range(nslab)]
                for h in range(nh):
                    r0 = h * nk + jh * jrows
                    s2 = s2_ref[lt, r0:r0 + jrows, :]
                    e2 = e2_ref[lt, r0:r0 + jrows, :]
                    for ii in range(nslab):
                        r = ii * nh + h
                        sel = s2 >= c1_ref[lt, r:r + 1, :]
                        w[ii] = w[ii] + jnp.where(sel, e2 * g1_ref[lt, r:r + 1, :], 0.0)
                for ii in range(nslab):
                    r0 = ii * nk + jh * jrows
                    a = _gelu(hid_ref[lt, r0:r0 + jrows, :]) * w[ii]
                    a_ref[r0:r0 + jrows, lt * LANES:(lt + 1) * LANES] = a.astype(BF16)

    nchunk = tm // cw
    hidden(0)
    for c in range(nchunk):
        if c + 1 < nchunk:
            hidden(c + 1)
        gate(c)
        out_ref[:, c * cw:(c + 1) * cw] += _dot(vt_ref[...], a_ref[:, c * cw:(c + 1) * cw])


def _peer_mix(xt, u_bf, vt_bf, s2, e2, c1, g1):
    d, t = xt.shape
    n_exp = u_bf.shape[0]
    nt, rows, _ = s2.shape
    tm = min(t, 1024)
    te = 512
    cw = min(tm, 256)
    crow = te // PEER_N_KEYS * PEER_HEADS
    once = pl.Buffered(1)
    tok = pl.BlockSpec((tm // LANES, rows, LANES), lambda i, e: (i, 0, 0), pipeline_mode=once)
    sel = pl.BlockSpec((tm // LANES, crow, LANES), lambda i, e: (i, e, 0))
    return pl.pallas_call(
        functools.partial(_peer_mix_kernel, te=te, tm=tm, cw=cw),
        grid=(t // tm, n_exp // te),
        in_specs=[pl.BlockSpec((d, tm), lambda i, e: (0, i), pipeline_mode=once),
                  pl.BlockSpec((te, d), lambda i, e: (e, 0)),
                  pl.BlockSpec((d, te), lambda i, e: (0, e)),
                  tok, tok, sel, sel],
        out_specs=pl.BlockSpec((d, tm), lambda i, e: (0, i)),
        out_shape=jax.ShapeDtypeStruct((d, t), F32),
        scratch_shapes=[pltpu.VMEM((tm // LANES, te, LANES), F32), pltpu.VMEM((te, tm), BF16)],
        compiler_params=_cparams("parallel", "arbitrary"),
        name="peer_mix",
    )(xt, u_bf, vt_bf, s2, e2, c1, g1)


def _ple_kernel(h_ref, po_ref, g_ref, wg_ref, p_ref, wp_ref, fn_ref, out_ref, *, final):
    h = h_ref[...] + po_ref[...]
    hn = _rms(h, g_ref[...]).astype(BF16)
    gate = jax.nn.sigmoid(_dot(hn, wg_ref[...]))
    emb = _dot(p_ref[...].astype(BF16), wp_ref[...])
    y = h + emb * gate
    if final:
        y = _rms(y, fn_ref[...])
    out_ref[...] = y


def _ple(h2, peer_out, g, wg_bf, p2, wp_bf, fn, final):
    t, d = h2.shape
    tm = min(t, 256)
    rd = pl.BlockSpec((tm, d), lambda i: (i, 0))
    one = pl.BlockSpec((1, d), lambda i: (0, 0))
    full = lambda a: pl.BlockSpec(a.shape, lambda i: (0, 0))
    return pl.pallas_call(
        functools.partial(_ple_kernel, final=final),
        grid=(t // tm,),
        in_specs=[rd, rd, one, full(wg_bf), pl.BlockSpec((tm, p2.shape[1]), lambda i: (i, 0)), full(wp_bf), one],
        out_specs=rd,
        out_shape=jax.ShapeDtypeStruct((t, d), F32),
        compiler_params=_cparams("parallel"),
        name="ple_final" if final else "ple",
    )(h2, peer_out, g.reshape(1, d), wg_bf, p2, wp_bf, fn.reshape(1, d))


def _subkey_matrices(sub):
    nh, _, nk, kd = sub.shape
    eye_h = jnp.eye(nh, dtype=sub.dtype)
    eye_c = jnp.eye(2, dtype=sub.dtype)
    knh = jnp.einsum("hcnd,hg,ce->cnhged", sub, eye_h, eye_c).reshape(2 * nk * nh, nh * 2 * kd)
    khn = jnp.einsum("hnd,hg,e->hnged", sub[:, 1], eye_h, eye_c[1]).reshape(nh * nk, nh * 2 * kd)
    return knh.astype(BF16), khn.astype(BF16)


def kernel(x, p, positions, attn_norm, w_in, out_norm_a, out_norm_b, w_out, ffn_norm, peer_wq, peer_subkeys,
           peer_u, peer_v, ple_norm, ple_gate, ple_proj, final_norm):
    b, s, d = x.shape
    depth = w_in.shape[0]
    t = b * s
    assert s % (LANES * max(dil for _, dil in DILATED_PATTERNS)) == 0
    assert all(win // dil == HEAD_DIM for win, dil in DILATED_PATTERNS)
    cos, sin = _rope_tables(positions)
    h = x.reshape(t, d)
    for i in range(depth):
        proj = _in_proj(h, attn_norm[i], w_in[i].astype(BF16), cos, sin)
        outs = [_band_attn(proj, b, s, dil, hg) for (_, dil), hg in zip(DILATED_PATTERNS, (2, 8, 8))]
        ob = _sb_attn(proj, b, s)
        h = _mix_out([o for o, _ in outs], [l for _, l in outs], ob, out_norm_a[i], out_norm_b[i],
                     w_out[i].astype(BF16), h)
        knh, khn = _subkey_matrices(peer_subkeys[i])
        xn, scnh, s2 = _peer_scores(h, ffn_norm[i], peer_wq[i].astype(BF16), knh, khn)
        c1, g1, e2 = _peer_select(scnh, s2)
        peer_t = _peer_mix(xn.T, peer_u[i].astype(BF16), peer_v[i].T.astype(BF16), s2, e2, c1, g1)
        h = _ple(h, peer_t.T, ple_norm[i], ple_gate[i].astype(BF16), p[i].reshape(t, -1),
                 ple_proj[i].astype(BF16), final_norm, final=(i == depth - 1))
    return h.reshape(b, s, d)
```

```python
import functools
import math

import jax
import jax.numpy as jnp
from jax import lax
from jax.experimental import pallas as pl
from jax.experimental.pallas import tpu as pltpu

F32 = jnp.float32
BF16 = jnp.bfloat16

HEAD_DIM = 128
N_HEADS_A = 8
N_HEADS_B = 8
WIDTH_A = N_HEADS_A * HEAD_DIM
WIDTH_B = N_HEADS_B * HEAD_DIM
DILATED_PATTERNS = ((128, 1), (512, 4), (2048, 16))
ROPE_THETA = 10000.0
PEER_HEADS = 8
PEER_N_KEYS = 128
PEER_TOPK = 16
NORM_EPS = 1e-6
NEG_INF = -1e30

LANES = 128
SUBLANES = 8
VMEM_LIMIT = 56 * 1024 * 1024
F32_EXP_ZERO = -104.0


def _cparams(*sem):
    return pltpu.CompilerParams(dimension_semantics=sem, vmem_limit_bytes=VMEM_LIMIT)


def _dot(a, b):
    return jnp.dot(a, b, preferred_element_type=F32)


def _dot_nt(a, b):
    return lax.dot_general(a, b, (((1,), (1,)), ((), ())), preferred_element_type=F32)


def _rms(x, g):
    return x * lax.rsqrt(jnp.mean(x * x, axis=-1, keepdims=True) + NORM_EPS) * g


def _rope_kernel(pos_ref, freq_ref, sign_ref, cos_ref, sin_ref):
    ang = pos_ref[...].astype(F32) * freq_ref[...]
    cos_ref[...] = jnp.cos(ang)
    sin_ref[...] = jnp.sin(ang) * sign_ref[...]


def _rope_tables(positions):
    t = positions.size
    half = HEAD_DIM // 2
    inv_freq = ROPE_THETA ** (-jnp.arange(half, dtype=F32) / half)
    freq = jnp.concatenate([inv_freq, inv_freq]).reshape(1, HEAD_DIM)
    sign = jnp.concatenate([-jnp.ones((half,), F32), jnp.ones((half,), F32)]).reshape(1, HEAD_DIM)
    tm = min(t, 1024)
    row = pl.BlockSpec((tm, HEAD_DIM), lambda i: (i, 0))
    one = pl.BlockSpec((1, HEAD_DIM), lambda i: (0, 0))
    return pl.pallas_call(
        _rope_kernel,
        grid=(t // tm,),
        in_specs=[pl.BlockSpec((tm, 1), lambda i: (i, 0)), one, one],
        out_specs=[row, row],
        out_shape=[jax.ShapeDtypeStruct((t, HEAD_DIM), F32)] * 2,
        compiler_params=_cparams("parallel"),
        name="rope_tables",
    )(positions.reshape(t, 1), freq, sign)


def _in_proj_kernel(x_ref, g_ref, w_ref, cos_ref, sin_ref, pa_ref, pb_ref, xn_ref, *, scale):
    j = pl.program_id(1)

    @pl.when(j == 0)
    def _():
        xn_ref[...] = _rms(x_ref[...], g_ref[...]).astype(BF16)

    acc = _dot(xn_ref[...], w_ref[...])

    @pl.when(j < 2)
    def _():
        sc = jnp.where(j == 0, scale, 1.0).astype(F32)
        c = cos_ref[...] * sc
        s = sin_ref[...] * sc
        for hh in range(N_HEADS_A):
            cs = slice(hh * HEAD_DIM, (hh + 1) * HEAD_DIM)
            a = acc[:, cs]
            pa_ref[:, cs] = a * c + pltpu.roll(a, HEAD_DIM // 2, 1) * s

    @pl.when(j == 2)
    def _():
        pa_ref[...] = acc

    @pl.when(j >= 3)
    def _():
        sc = jnp.where(j == 3, scale, 1.0).astype(F32)
        pb_ref[...] = (acc * sc).astype(BF16)


def _in_proj(h2, g, w_bf, cos, sin):
    t, d = h2.shape
    n = w_bf.shape[1]
    tm = min(t, 512)
    tn = WIDTH_A
    na = 3
    return pl.pallas_call(
        functools.partial(_in_proj_kernel, scale=HEAD_DIM ** -0.5),
        grid=(t // tm, n // tn),
        in_specs=[
            pl.BlockSpec((tm, d), lambda i, j: (i, 0)),
            pl.BlockSpec((1, d), lambda i, j: (0, 0)),
            pl.BlockSpec((d, tn), lambda i, j: (0, j)),
            pl.BlockSpec((tm, HEAD_DIM), lambda i, j: (i, 0)),
            pl.BlockSpec((tm, HEAD_DIM), lambda i, j: (i, 0)),
        ],
        out_specs=[pl.BlockSpec((tm, tn), lambda i, j: (i, jnp.minimum(j, na - 1))),
                   pl.BlockSpec((tm, tn), lambda i, j: (i, jnp.maximum(j - na, 0)))],
        out_shape=[jax.ShapeDtypeStruct((t, na * tn), F32), jax.ShapeDtypeStruct((t, n - na * tn), BF16)],
        scratch_shapes=[pltpu.VMEM((tm, d), BF16)],
        compiler_params=_cparams("parallel", "arbitrary"),
        name="in_proj",
    )(h2, g.reshape(1, d), w_bf, cos, sin)


def _dilated_kernel(q_ref, k_ref, v_ref, o_ref, oacc, lacc, *, seq):
    w = HEAD_DIM
    diff = (lax.broadcasted_iota(jnp.int32, (w, 2 * w), 0) - lax.broadcasted_iota(jnp.int32, (w, 2 * w), 1))
    ones = jnp.ones((2 * w, w), BF16)

    def block(blk, r, d, first):
        span = w * d
        pblk = jnp.maximum(blk - 1, 0)
        qwin = pl.ds(pl.multiple_of(blk * span, span), span)
        kwin = pl.ds(pl.multiple_of(pblk * span, span), 2 * span)
        rows = (lambda n: pl.ds(r, n, stride=d)) if d > 1 else (lambda n: pl.ds(0, n))
        valid = jnp.abs(diff + ((blk - pblk) * w - w // 2)) <= w // 2
        q = q_ref.at[qwin][rows(w), :].astype(BF16)
        k2 = k_ref.at[kwin][rows(2 * w), :].astype(BF16)
        v2 = v_ref.at[kwin][rows(2 * w), :].astype(BF16)
        s = jnp.where(valid, _dot_nt(q, k2), NEG_INF)
        m = jnp.max(s, axis=-1, keepdims=True)
        e = jnp.exp(s - m).astype(BF16)
        pv = _dot(e, jnp.concatenate([v2, ones], axis=1))
        l = pv[:, w:]
        o = pv[:, :w] / l
        lse = m + jnp.log(l)
        if first:
            oacc.at[qwin][rows(w), :] = o
            lacc.at[qwin][rows(w), :] = lse
        else:
            o0 = oacc.at[qwin][rows(w), :]
            l0 = lacc.at[qwin][rows(w), :]
            mx = jnp.maximum(l0, lse)
            e0 = jnp.exp(l0 - mx)
            e1 = jnp.exp(lse - mx)
            den = e0 + e1
            oacc.at[qwin][rows(w), :] = (e0 * o0 + e1 * o) / den
            lacc.at[qwin][rows(w), :] = mx + jnp.log(den)

    for pi, (_, d) in enumerate(DILATED_PATTERNS):
        nblk = seq // d // w
        ub = max(1, 4 // d)

        def body(g, carry, d=d, ub=ub, first=(pi == 0)):
            for u in range(ub):
                for r in range(d):
                    block(g * ub + u, r, d, first)
            return carry

        lax.fori_loop(0, nblk // ub, body, 0)
    o_ref[...] = oacc[...].astype(BF16)


def _dilated_attn(pa, b, s):
    view = pa.reshape(b, s, pa.shape[1])

    def in_map(x):
        return lambda bi, h: (bi, 0, x * N_HEADS_A + h)

    spec = lambda x: pl.BlockSpec((None, s, HEAD_DIM), in_map(x))
    o = pl.pallas_call(
        functools.partial(_dilated_kernel, seq=s),
        grid=(b, N_HEADS_A),
        in_specs=[spec(0), spec(1), spec(2)],
        out_specs=pl.BlockSpec((None, s, HEAD_DIM), lambda bi, h: (bi, 0, h)),
        out_shape=jax.ShapeDtypeStruct((b, s, WIDTH_A), BF16),
        scratch_shapes=[pltpu.VMEM((s, HEAD_DIM), F32), pltpu.VMEM((s, HEAD_DIM), F32)],
        compiler_params=_cparams("parallel", "parallel"),
        name="dilated_attn",
    )(view, view, view)
    return o.reshape(b * s, WIDTH_A)


def _sb_kernel(q_ref, k_ref, v_ref, o_ref, *, blk, nq, hg):
    row = lax.broadcasted_iota(jnp.int32, (blk, blk), 0)
    col = lax.broadcasted_iota(jnp.int32, (blk, blk), 1)
    tri = jnp.where(row > col, 1.0, 0.0).astype(BF16)
    heads = [slice(h * HEAD_DIM, (h + 1) * HEAD_DIM) for h in range(hg)]

    def qblock(i, carry):
        q0 = pl.multiple_of(i * blk, blk)
        qs = [q_ref[pl.ds(q0, blk), cs] for cs in heads]

        def cond(st):
            kb, cs_, _ = st
            return jnp.logical_and(kb >= 0, jnp.max(functools.reduce(jnp.maximum, cs_)) > F32_EXP_ZERO)

        def body(st):
            kb, cs_, accs = st
            k0 = pl.multiple_of(kb * blk, blk)
            valid = jnp.logical_or(kb < i, col < row)
            new_c, new_acc = [], []
            for q, c, acc, cs in zip(qs, cs_, accs, heads):
                z = _dot_nt(q, k_ref[pl.ds(k0, blk), cs])
                sp = jnp.maximum(z, 0.0) + jnp.log(1.0 + jnp.exp(-jnp.abs(z)))
                lm = jnp.where(valid, -sp, 0.0)
                hi = lm.astype(BF16)
                lo = (lm - hi.astype(F32)).astype(BF16)
                later = _dot(hi, tri) + _dot(lo, tri)
                a = jnp.where(valid, jnp.exp(z - sp + later + c), 0.0)
                new_acc.append(acc + _dot(a.astype(BF16), v_ref[pl.ds(k0, blk), cs]))
                new_c.append(c + jnp.sum(lm, axis=-1, keepdims=True))
            return kb - 1, tuple(new_c), tuple(new_acc)

        init = (i, tuple(jnp.zeros((blk, 1), F32) for _ in heads),
                tuple(jnp.zeros((blk, HEAD_DIM), F32) for _ in heads))
        _, _, accs = lax.while_loop(cond, body, init)
        for acc, cs in zip(accs, heads):
            o_ref[pl.ds(q0, blk), cs] = acc.astype(BF16)
        return carry

    lax.fori_loop(0, nq, qblock, 0)


def _sb_attn(pb, b, s):
    view = pb.reshape(b, s, pb.shape[1])
    blk = min(s, 256)
    hg = 4
    cw = hg * HEAD_DIM
    ngrp = N_HEADS_B // hg

    def in_map(x):
        return lambda bi, g: (bi, 0, x * ngrp + g)

    o = pl.pallas_call(
        functools.partial(_sb_kernel, blk=blk, nq=s // blk, hg=hg),
        grid=(b, ngrp),
        in_specs=[pl.BlockSpec((None, s, cw), in_map(0)),
                  pl.BlockSpec((None, s, cw), in_map(1)),
                  pl.BlockSpec((None, s, cw), in_map(2))],
        out_specs=pl.BlockSpec((None, s, cw), lambda bi, g: (bi, 0, g)),
        out_shape=jax.ShapeDtypeStruct((b, s, WIDTH_B), BF16),
        compiler_params=_cparams("parallel", "parallel"),
        name="stick_breaking",
    )(view, view, view)
    return o.reshape(b * s, WIDTH_B)


def _mix_out_kernel(oa_ref, ob_ref, ga_ref, gb_ref, w_ref, h_ref, out_ref):
    na = _rms(oa_ref[...].astype(F32), ga_ref[...]).astype(BF16)
    nb = _rms(ob_ref[...].astype(F32), gb_ref[...]).astype(BF16)
    y = _dot(na, w_ref[:WIDTH_A, :]) + _dot(nb, w_ref[WIDTH_A:, :])
    out_ref[...] = h_ref[...] + y


def _mix_out(oa, ob, ga, gb, w_bf, h2):
    t, d = h2.shape
    tm = min(t, 512)
    ra = pl.BlockSpec((tm, WIDTH_A), lambda i: (i, 0))
    rd = pl.BlockSpec((tm, d), lambda i: (i, 0))
    return pl.pallas_call(
        _mix_out_kernel,
        grid=(t // tm,),
        in_specs=[ra, ra,
                  pl.BlockSpec((1, WIDTH_A), lambda i: (0, 0)),
                  pl.BlockSpec((1, WIDTH_B), lambda i: (0, 0)),
                  pl.BlockSpec(w_bf.shape, lambda i: (0, 0)),
                  rd],
        out_specs=rd,
        out_shape=jax.ShapeDtypeStruct((t, d), F32),
        compiler_params=_cparams("parallel"),
        name="mix_out_proj",
    )(oa, ob, ga.reshape(1, -1), gb.reshape(1, -1), w_bf, h2)


def _peer_scores_kernel(h_ref, g_ref, wq_ref, knh_ref, khn_ref, xn_ref, scnh_ref, s2_ref):
    xn = _rms(h_ref[...], g_ref[...]).astype(BF16)
    xn_ref[...] = xn
    qr = _dot(xn, wq_ref[...]).astype(BF16)
    scnh_ref[...] = _dot_nt(knh_ref[...], qr)
    s2 = _dot_nt(khn_ref[...], qr)
    for lt in range(s2_ref.shape[0]):
        s2_ref[lt] = s2[:, lt * LANES:(lt + 1) * LANES]


def _peer_scores(h2, g, wq_bf, knh, khn):
    t, d = h2.shape
    tm = min(t, 512)
    full = lambda a: pl.BlockSpec(a.shape, lambda i: (0, 0))
    return pl.pallas_call(
        _peer_scores_kernel,
        grid=(t // tm,),
        in_specs=[pl.BlockSpec((tm, d), lambda i: (i, 0)),
                  pl.BlockSpec((1, d), lambda i: (0, 0)),
                  full(wq_bf), full(knh), full(khn)],
        out_specs=[pl.BlockSpec((tm, d), lambda i: (i, 0)),
                   pl.BlockSpec((knh.shape[0], tm), lambda i: (0, i)),
                   pl.BlockSpec((tm // LANES, khn.shape[0], LANES), lambda i: (i, 0, 0))],
        out_shape=[jax.ShapeDtypeStruct((t, d), BF16),
                   jax.ShapeDtypeStruct((knh.shape[0], t), F32),
                   jax.ShapeDtypeStruct((t // LANES, khn.shape[0], LANES), F32)],
        compiler_params=_cparams("parallel"),
        name="peer_scores",
    )(h2, g.reshape(1, d), wq_bf, knh, khn)


def _oddeven_merge_sort_pairs(n):
    pairs = []

    def merge(lo, m, r):
        step = r * 2
        if step < m:
            merge(lo, m, step)
            merge(lo + r, m, step)
            for i in range(lo + r, lo + m - r, step):
                pairs.append((i, i + r))
        else:
            pairs.append((lo, lo + r))

    def sort(lo, m):
        if m > 1:
            h = m // 2
            sort(lo, h)
            sort(lo + h, h)
            merge(lo, m, 1)

    sort(0, n)
    return pairs


_SORT16 = _oddeven_merge_sort_pairs(PEER_TOPK)


def _sort_desc(vals):
    vals = list(vals)
    for i, j in _SORT16:
        a, b = vals[i], vals[j]
        vals[i], vals[j] = jnp.maximum(a, b), jnp.minimum(a, b)
    return vals


def _merge_top(a, b):
    n = len(a)
    c = [jnp.maximum(a[i], b[n - 1 - i]) for i in range(n)]
    stride = n // 2
    while stride >= 1:
        for i in range(n):
            if (i // stride) % 2 == 0:
                x, y = c[i], c[i + stride]
                c[i], c[i + stride] = jnp.maximum(x, y), jnp.minimum(x, y)
        stride //= 2
    return c


def _top_sorted(vals):
    k = PEER_TOPK
    groups = [_sort_desc(vals[i:i + k]) for i in range(0, len(vals), k)]
    while len(groups) > 1:
        groups = [_merge_top(groups[i], groups[i + 1]) for i in range(0, len(groups), 2)]
    return groups[0]


def _peer_select_kernel(scnh_ref, s2_ref, c1_ref, g1_ref, e2_ref, *, nchunk):
    nk, nh, k = PEER_N_KEYS, PEER_HEADS, PEER_TOPK
    big = -NEG_INF
    for lc in range(nchunk):
        ls = slice(lc * LANES, (lc + 1) * LANES)
        x1 = [scnh_ref[n * nh:(n + 1) * nh, ls] for n in range(nk)]
        x2 = [scnh_ref[(nk + n) * nh:(nk + n + 1) * nh, ls] for n in range(nk)]
        a = _top_sorted(x1)
        b = _top_sorted(x2)
        cand = [a[i] + b[j] for i in range(k) for j in range(k) if (i + 1) * (j + 1) <= k]
        pad = [jnp.full_like(a[0], NEG_INF)] * (-len(cand) % k)
        top = _top_sorted(cand + pad)
        c16 = top[k - 1]
        c17 = functools.reduce(jnp.maximum, [jnp.where(c < c16, c, NEG_INF) for c in cand])
        thr = 0.5 * (c16 + c17)
        mx = a[0] + b[0]
        z = functools.reduce(lambda u, v: u + v, [jnp.where(c > thr, jnp.exp(c - mx), 0.0) for c in cand])
        inv_z = 1.0 / z
        for n in range(nk):
            keep = x1[n] >= a[k - 1]
            c1_ref[lc, n * nh:(n + 1) * nh, :] = jnp.where(keep, thr - x1[n], big)
            g1_ref[lc, n * nh:(n + 1) * nh, :] = jnp.exp(x1[n] - a[0])
        for h in range(nh):
            s2 = s2_ref[lc, h * nk:(h + 1) * nk, :]
            e = jnp.exp(s2 - b[0][h:h + 1, :]) * inv_z[h:h + 1, :]
            e2_ref[lc, h * nk:(h + 1) * nk, :] = jnp.where(s2 >= b[k - 1][h:h + 1, :], e, 0.0)


def _peer_select(scnh, s2):
    nt, rows, _ = s2.shape
    nchunk = min(nt, 2)
    spec = pl.BlockSpec((nchunk, rows, LANES), lambda i: (i, 0, 0))
    return pl.pallas_call(
        functools.partial(_peer_select_kernel, nchunk=nchunk),
        grid=(nt // nchunk,),
        in_specs=[pl.BlockSpec((scnh.shape[0], nchunk * LANES), lambda i: (0, i)), spec],
        out_specs=[spec, spec, spec],
        out_shape=[jax.ShapeDtypeStruct(s2.shape, F32)] * 3,
        compiler_params=_cparams("parallel"),
        name="peer_select",
    )(scnh, s2)


def _gelu(x):
    return 0.5 * x * (1.0 + lax.erf(x * (2.0 ** -0.5)))


def _peer_mix_kernel(xt_ref, u_ref, vt_ref, s2_ref, e2_ref, c1_ref, g1_ref, out_ref, hid_ref, a_ref, *, te, tm, cw):
    nk, nh = PEER_N_KEYS, PEER_HEADS
    nslab = te // nk
    jrows = nk // 2
    tiles = cw // LANES

    @pl.when(pl.program_id(1) == 0)
    def _():
        out_ref[...] = jnp.zeros_like(out_ref)

    def hidden(c):
        hid = _dot(u_ref[...], xt_ref[:, c * cw:(c + 1) * cw])
        for l in range(tiles):
            hid_ref[c * tiles + l] = hid[:, l * LANES:(l + 1) * LANES]

    def gate(c):
        for l in range(tiles):
            lt = c * tiles + l
            for jh in range(2):
                w = [jnp.zeros((jrows, LANES), F32) for _ in range(nslab)]
                for h in range(nh):
                    r0 = h * nk + jh * jrows
                    s2 = s2_ref[lt, r0:r0 + jrows, :]
                    e2 = e2_ref[lt, r0:r0 + jrows, :]
                    for ii in range(nslab):
                        r = ii * nh + h
                        sel = s2 >= c1_ref[lt, r:r + 1, :]
                        w[ii] = w[ii] + jnp.where(sel, e2 * g1_ref[lt, r:r + 1, :], 0.0)
                for ii in range(nslab):
                    r0 = ii * nk + jh * jrows
                    a = _gelu(hid_ref[lt, r0:r0 + jrows, :]) * w[ii]
                    a_ref[r0:r0 + jrows, lt * LANES:(lt + 1) * LANES] = a.astype(BF16)

    nchunk = tm // cw
    hidden(0)
    for c in range(nchunk):
        if c + 1 < nchunk:
            hidden(c + 1)
        gate(c)
        out_ref[:, c * cw:(c + 1) * cw] += _dot(vt_ref[...], a_ref[:, c * cw:(c + 1) * cw])


def _peer_mix(xt, u_bf, vt_bf, s2, e2, c1, g1):
    d, t = xt.shape
    n_exp = u_bf.shape[0]
    nt, rows, _ = s2.shape
    tm = min(t, 1024)
    te = 512
    cw = min(tm, 256)
    crow = te // PEER_N_KEYS * PEER_HEADS
    once = pl.Buffered(1)
    tok = pl.BlockSpec((tm // LANES, rows, LANES), lambda i, e: (i, 0, 0), pipeline_mode=once)
    sel = pl.BlockSpec((tm // LANES, crow, LANES), lambda i, e: (i, e, 0))
    return pl.pallas_call(
        functools.partial(_peer_mix_kernel, te=te, tm=tm, cw=cw),
        grid=(t // tm, n_exp // te),
        in_specs=[pl.BlockSpec((d, tm), lambda i, e: (0, i), pipeline_mode=once),
                  pl.BlockSpec((te, d), lambda i, e: (e, 0)),
                  pl.BlockSpec((d, te), lambda i, e: (0, e)),
                  tok, tok, sel, sel],
        out_specs=pl.BlockSpec((d, tm), lambda i, e: (0, i)),
        out_shape=jax.ShapeDtypeStruct((d, t), F32),
        scratch_shapes=[pltpu.VMEM((tm // LANES, te, LANES), F32), pltpu.VMEM((te, tm), BF16)],
        compiler_params=_cparams("parallel", "arbitrary"),
        name="peer_mix",
    )(xt, u_bf, vt_bf, s2, e2, c1, g1)


def _ple_kernel(h_ref, po_ref, g_ref, wg_ref, p_ref, wp_ref, fn_ref, out_ref, *, final):
    h = h_ref[...] + po_ref[...]
    hn = _rms(h, g_ref[...]).astype(BF16)
    gate = jax.nn.sigmoid(_dot(hn, wg_ref[...]))
    emb = _dot(p_ref[...].astype(BF16), wp_ref[...])
    y = h + emb * gate
    if final:
        y = _rms(y, fn_ref[...])
    out_ref[...] = y


def _ple(h2, peer_out, g, wg_bf, p2, wp_bf, fn, final):
    t, d = h2.shape
    tm = min(t, 256)
    rd = pl.BlockSpec((tm, d), lambda i: (i, 0))
    one = pl.BlockSpec((1, d), lambda i: (0, 0))
    full = lambda a: pl.BlockSpec(a.shape, lambda i: (0, 0))
    return pl.pallas_call(
        functools.partial(_ple_kernel, final=final),
        grid=(t // tm,),
        in_specs=[rd, rd, one, full(wg_bf), pl.BlockSpec((tm, p2.shape[1]), lambda i: (i, 0)), full(wp_bf), one],
        out_specs=rd,
        out_shape=jax.ShapeDtypeStruct((t, d), F32),
        compiler_params=_cparams("parallel"),
        name="ple_final" if final else "ple",
    )(h2, peer_out, g.reshape(1, d), wg_bf, p2, wp_bf, fn.reshape(1, d))


def _subkey_matrices(sub):
    nh, _, nk, kd = sub.shape
    eye_h = jnp.eye(nh, dtype=sub.dtype)
    eye_c = jnp.eye(2, dtype=sub.dtype)
    knh = jnp.einsum("hcnd,hg,ce->cnhged", sub, eye_h, eye_c).reshape(2 * nk * nh, nh * 2 * kd)
    khn = jnp.einsum("hnd,hg,e->hnged", sub[:, 1], eye_h, eye_c[1]).reshape(nh * nk, nh * 2 * kd)
    return knh.astype(BF16), khn.astype(BF16)


def kernel(x, p, positions, attn_norm, w_in, out_norm_a, out_norm_b, w_out, ffn_norm, peer_wq, peer_subkeys,
           peer_u, peer_v, ple_norm, ple_gate, ple_proj, final_norm):
    b, s, d = x.shape
    depth = w_in.shape[0]
    t = b * s
    assert s % (LANES * max(dil for _, dil in DILATED_PATTERNS)) == 0
    assert all(win // dil == HEAD_DIM for win, dil in DILATED_PATTERNS)
    cos, sin = _rope_tables(positions)
    h = x.reshape(t, d)
    for i in range(depth):
        pa, pb = _in_proj(h, attn_norm[i], w_in[i].astype(BF16), cos, sin)
        oa = _dilated_attn(pa, b, s)
        ob = _sb_attn(pb, b, s)
        h = _mix_out(oa, ob, out_norm_a[i], out_norm_b[i], w_out[i].astype(BF16), h)
        knh, khn = _subkey_matrices(peer_subkeys[i])
        xn, scnh, s2 = _peer_scores(h, ffn_norm[i], peer_wq[i].astype(BF16), knh, khn)
        c1, g1, e2 = _peer_select(scnh, s2)
        peer_t = _peer_mix(xn.T, peer_u[i].astype(BF16), peer_v[i].T.astype(BF16), s2, e2, c1, g1)
        h = _ple(h, peer_t.T, ple_norm[i], ple_gate[i].astype(BF16), p[i].reshape(t, -1),
                 ple_proj[i].astype(BF16), final_norm, final=(i == depth - 1))
    return h.reshape(b, s, d)
```

```python
import functools
import math

import jax
import jax.numpy as jnp
from jax import lax
from jax.experimental import pallas as pl
from jax.experimental.pallas import tpu as pltpu

F32 = jnp.float32
BF16 = jnp.bfloat16

HEAD_DIM = 128
N_HEADS_A = 8
N_HEADS_B = 8
WIDTH_A = N_HEADS_A * HEAD_DIM
WIDTH_B = N_HEADS_B * HEAD_DIM
DILATED_PATTERNS = ((128, 1), (512, 4), (2048, 16))
ROPE_THETA = 10000.0
PEER_HEADS = 8
PEER_N_KEYS = 128
PEER_TOPK = 16
NORM_EPS = 1e-6
NEG_INF = -1e30

LANES = 128
SUBLANES = 8
VMEM_LIMIT = 56 * 1024 * 1024
F32_EXP_ZERO = -104.0


def _cparams(*sem):
    return pltpu.CompilerParams(dimension_semantics=sem, vmem_limit_bytes=VMEM_LIMIT)


def _dot(a, b):
    return jnp.dot(a, b, preferred_element_type=F32)


def _dot_nt(a, b):
    return lax.dot_general(a, b, (((1,), (1,)), ((), ())), preferred_element_type=F32)


def _rms(x, g):
    return x * lax.rsqrt(jnp.mean(x * x, axis=-1, keepdims=True) + NORM_EPS) * g


def _rope_kernel(pos_ref, freq_ref, sign_ref, cos_ref, sin_ref):
    ang = pos_ref[...].astype(F32) * freq_ref[...]
    cos_ref[...] = jnp.cos(ang)
    sin_ref[...] = jnp.sin(ang) * sign_ref[...]


def _rope_tables(positions):
    t = positions.size
    half = HEAD_DIM // 2
    inv_freq = ROPE_THETA ** (-jnp.arange(half, dtype=F32) / half)
    freq = jnp.concatenate([inv_freq, inv_freq]).reshape(1, HEAD_DIM)
    sign = jnp.concatenate([-jnp.ones((half,), F32), jnp.ones((half,), F32)]).reshape(1, HEAD_DIM)
    tm = min(t, 1024)
    row = pl.BlockSpec((tm, HEAD_DIM), lambda i: (i, 0))
    one = pl.BlockSpec((1, HEAD_DIM), lambda i: (0, 0))
    return pl.pallas_call(
        _rope_kernel,
        grid=(t // tm,),
        in_specs=[pl.BlockSpec((tm, 1), lambda i: (i, 0)), one, one],
        out_specs=[row, row],
        out_shape=[jax.ShapeDtypeStruct((t, HEAD_DIM), F32)] * 2,
        compiler_params=_cparams("parallel"),
        name="rope_tables",
    )(positions.reshape(t, 1), freq, sign)


def _in_proj_kernel(x_ref, g_ref, w_ref, cos_ref, sin_ref, pa_ref, pb_ref, xn_ref, *, scale):
    j = pl.program_id(1)

    @pl.when(j == 0)
    def _():
        xn_ref[...] = _rms(x_ref[...], g_ref[...]).astype(BF16)

    acc = _dot(xn_ref[...], w_ref[...])

    @pl.when(j < 2)
    def _():
        sc = jnp.where(j == 0, scale, 1.0).astype(F32)
        c = cos_ref[...] * sc
        s = sin_ref[...] * sc
        for hh in range(N_HEADS_A):
            cs = slice(hh * HEAD_DIM, (hh + 1) * HEAD_DIM)
            a = acc[:, cs]
            pa_ref[:, cs] = a * c + pltpu.roll(a, HEAD_DIM // 2, 1) * s

    @pl.when(j == 2)
    def _():
        pa_ref[...] = acc

    @pl.when(j >= 3)
    def _():
        sc = jnp.where(j == 3, scale, 1.0).astype(F32)
        pb_ref[...] = (acc * sc).astype(BF16)


def _in_proj(h2, g, w_bf, cos, sin):
    t, d = h2.shape
    n = w_bf.shape[1]
    tm = min(t, 512)
    tn = WIDTH_A
    na = 3
    return pl.pallas_call(
        functools.partial(_in_proj_kernel, scale=HEAD_DIM ** -0.5),
        grid=(t // tm, n // tn),
        in_specs=[
            pl.BlockSpec((tm, d), lambda i, j: (i, 0)),
            pl.BlockSpec((1, d), lambda i, j: (0, 0)),
            pl.BlockSpec((d, tn), lambda i, j: (0, j)),
            pl.BlockSpec((tm, HEAD_DIM), lambda i, j: (i, 0)),
            pl.BlockSpec((tm, HEAD_DIM), lambda i, j: (i, 0)),
        ],
        out_specs=[pl.BlockSpec((tm, tn), lambda i, j: (i, jnp.minimum(j, na - 1))),
                   pl.BlockSpec((tm, tn), lambda i, j: (i, jnp.maximum(j - na, 0)))],
        out_shape=[jax.ShapeDtypeStruct((t, na * tn), F32), jax.ShapeDtypeStruct((t, n - na * tn), BF16)],
        scratch_shapes=[pltpu.VMEM((tm, d), BF16)],
        compiler_params=_cparams("parallel", "arbitrary"),
        name="in_proj",
    )(h2, g.reshape(1, d), w_bf, cos, sin)


def _dilated_kernel(q_ref, k_ref, v_ref, o_ref, oacc, lacc, *, seq):
    w = HEAD_DIM
    diff = (lax.broadcasted_iota(jnp.int32, (w, 2 * w), 0) - lax.broadcasted_iota(jnp.int32, (w, 2 * w), 1))
    ones = jnp.ones((2 * w, w), BF16)

    def block(blk, r, d, first):
        span = w * d
        pblk = jnp.maximum(blk - 1, 0)
        qwin = pl.ds(pl.multiple_of(blk * span, span), span)
        kwin = pl.ds(pl.multiple_of(pblk * span, span), 2 * span)
        rows = (lambda n: pl.ds(r, n, stride=d)) if d > 1 else (lambda n: pl.ds(0, n))
        valid = jnp.abs(diff + ((blk - pblk) * w - w // 2)) <= w // 2
        q = q_ref.at[qwin][rows(w), :].astype(BF16)
        k2 = k_ref.at[kwin][rows(2 * w), :].astype(BF16)
        v2 = v_ref.at[kwin][rows(2 * w), :].astype(BF16)
        s = jnp.where(valid, _dot_nt(q, k2), NEG_INF)
        m = jnp.max(s, axis=-1, keepdims=True)
        e = jnp.exp(s - m).astype(BF16)
        pv = _dot(e, jnp.concatenate([v2, ones], axis=1))
        l = pv[:, w:]
        o = pv[:, :w] / l
        lse = m + jnp.log(l)
        if first:
            oacc.at[qwin][rows(w), :] = o
            lacc.at[qwin][rows(w), :] = lse
        else:
            o0 = oacc.at[qwin][rows(w), :]
            l0 = lacc.at[qwin][rows(w), :]
            mx = jnp.maximum(l0, lse)
            e0 = jnp.exp(l0 - mx)
            e1 = jnp.exp(lse - mx)
            den = e0 + e1
            oacc.at[qwin][rows(w), :] = (e0 * o0 + e1 * o) / den
            lacc.at[qwin][rows(w), :] = mx + jnp.log(den)

    for pi, (_, d) in enumerate(DILATED_PATTERNS):
        nblk = seq // d // w
        ub = max(1, 4 // d)

        def body(g, carry, d=d, ub=ub, first=(pi == 0)):
            for u in range(ub):
                for r in range(d):
                    block(g * ub + u, r, d, first)
            return carry

        lax.fori_loop(0, nblk // ub, body, 0)
    o_ref[...] = oacc[...].astype(BF16)


def _dilated_attn(pa, b, s):
    view = pa.reshape(b, s, pa.shape[1])

    def in_map(x):
        return lambda bi, h: (bi, 0, x * N_HEADS_A + h)

    spec = lambda x: pl.BlockSpec((None, s, HEAD_DIM), in_map(x))
    o = pl.pallas_call(
        functools.partial(_dilated_kernel, seq=s),
        grid=(b, N_HEADS_A),
        in_specs=[spec(0), spec(1), spec(2)],
        out_specs=pl.BlockSpec((None, s, HEAD_DIM), lambda bi, h: (bi, 0, h)),
        out_shape=jax.ShapeDtypeStruct((b, s, WIDTH_A), BF16),
        scratch_shapes=[pltpu.VMEM((s, HEAD_DIM), F32), pltpu.VMEM((s, HEAD_DIM), F32)],
        compiler_params=_cparams("parallel", "parallel"),
        name="dilated_attn",
    )(view, view, view)
    return o.reshape(b * s, WIDTH_A)


def _sb_kernel(q_ref, k_ref, v_ref, o_ref, *, blk, nq, hg):
    row = lax.broadcasted_iota(jnp.int32, (blk, blk), 0)
    col = lax.broadcasted_iota(jnp.int32, (blk, blk), 1)
    tri = jnp.where(row > col, 1.0, 0.0).astype(BF16)
    heads = [slice(h * HEAD_DIM, (h + 1) * HEAD_DIM) for h in range(hg)]

    def qblock(i, carry):
        q0 = pl.multiple_of(i * blk, blk)
        qs = [q_ref[pl.ds(q0, blk), cs] for cs in heads]

        def cond(st):
            kb, cs_, _ = st
            return jnp.logical_and(kb >= 0, jnp.max(functools.reduce(jnp.maximum, cs_)) > F32_EXP_ZERO)

        def body(st):
            kb, cs_, accs = st
            k0 = pl.multiple_of(kb * blk, blk)
            valid = jnp.logical_or(kb < i, col < row)
            new_c, new_acc = [], []
            for q, c, acc, cs in zip(qs, cs_, accs, heads):
                z = _dot_nt(q, k_ref[pl.ds(k0, blk), cs])
                sp = jnp.maximum(z, 0.0) + jnp.log(1.0 + jnp.exp(-jnp.abs(z)))
                lm = jnp.where(valid, -sp, 0.0)
                hi = lm.astype(BF16)
                lo = (lm - hi.astype(F32)).astype(BF16)
                later = _dot(hi, tri) + _dot(lo, tri)
                a = jnp.where(valid, jnp.exp(z - sp + later + c), 0.0)
                new_acc.append(acc + _dot(a.astype(BF16), v_ref[pl.ds(k0, blk), cs]))
                new_c.append(c + jnp.sum(lm, axis=-1, keepdims=True))
            return kb - 1, tuple(new_c), tuple(new_acc)

        init = (i, tuple(jnp.zeros((blk, 1), F32) for _ in heads),
                tuple(jnp.zeros((blk, HEAD_DIM), F32) for _ in heads))
        _, _, accs = lax.while_loop(cond, body, init)
        for acc, cs in zip(accs, heads):
            o_ref[pl.ds(q0, blk), cs] = acc.astype(BF16)
        return carry

    lax.fori_loop(0, nq, qblock, 0)


def _sb_attn(pb, b, s):
    view = pb.reshape(b, s, pb.shape[1])
    blk = min(s, 256)
    hg = 4
    cw = hg * HEAD_DIM
    ngrp = N_HEADS_B // hg

    def in_map(x):
        return lambda bi, g: (bi, 0, x * ngrp + g)

    o = pl.pallas_call(
        functools.partial(_sb_kernel, blk=blk, nq=s // blk, hg=hg),
        grid=(b, ngrp),
        in_specs=[pl.BlockSpec((None, s, cw), in_map(0)),
                  pl.BlockSpec((None, s, cw), in_map(1)),
                  pl.BlockSpec((None, s, cw), in_map(2))],
        out_specs=pl.BlockSpec((None, s, cw), lambda bi, g: (bi, 0, g)),
        out_shape=jax.ShapeDtypeStruct((b, s, WIDTH_B), BF16),
        compiler_params=_cparams("parallel", "parallel"),
        name="stick_breaking",
    )(view, view, view)
    return o.reshape(b * s, WIDTH_B)


def _mix_out_kernel(oa_ref, ob_ref, ga_ref, gb_ref, w_ref, h_ref, out_ref):
    na = _rms(oa_ref[...].astype(F32), ga_ref[...]).astype(BF16)
    nb = _rms(ob_ref[...].astype(F32), gb_ref[...]).astype(BF16)
    y = _dot(na, w_ref[:WIDTH_A, :]) + _dot(nb, w_ref[WIDTH_A:, :])
    out_ref[...] = h_ref[...] + y


def _mix_out(oa, ob, ga, gb, w_bf, h2):
    t, d = h2.shape
    tm = min(t, 512)
    ra = pl.BlockSpec((tm, WIDTH_A), lambda i: (i, 0))
    rd = pl.BlockSpec((tm, d), lambda i: (i, 0))
    return pl.pallas_call(
        _mix_out_kernel,
        grid=(t // tm,),
        in_specs=[ra, ra,
                  pl.BlockSpec((1, WIDTH_A), lambda i: (0, 0)),
                  pl.BlockSpec((1, WIDTH_B), lambda i: (0, 0)),
                  pl.BlockSpec(w_bf.shape, lambda i: (0, 0)),
                  rd],
        out_specs=rd,
        out_shape=jax.ShapeDtypeStruct((t, d), F32),
        compiler_params=_cparams("parallel"),
        name="mix_out_proj",
    )(oa, ob, ga.reshape(1, -1), gb.reshape(1, -1), w_bf, h2)


def _peer_scores_kernel(h_ref, g_ref, wq_ref, knh_ref, khn_ref, xn_ref, scnh_ref, s2_ref):
    xn = _rms(h_ref[...], g_ref[...]).astype(BF16)
    xn_ref[...] = xn
    qr = _dot(xn, wq_ref[...]).astype(BF16)
    scnh_ref[...] = _dot_nt(knh_ref[...], qr)
    s2 = _dot_nt(khn_ref[...], qr)
    for lt in range(s2_ref.shape[0]):
        s2_ref[lt] = s2[:, lt * LANES:(lt + 1) * LANES]


def _peer_scores(h2, g, wq_bf, knh, khn):
    t, d = h2.shape
    tm = min(t, 512)
    full = lambda a: pl.BlockSpec(a.shape, lambda i: (0, 0))
    return pl.pallas_call(
        _peer_scores_kernel,
        grid=(t // tm,),
        in_specs=[pl.BlockSpec((tm, d), lambda i: (i, 0)),
                  pl.BlockSpec((1, d), lambda i: (0, 0)),
                  full(wq_bf), full(knh), full(khn)],
        out_specs=[pl.BlockSpec((tm, d), lambda i: (i, 0)),
                   pl.BlockSpec((knh.shape[0], tm), lambda i: (0, i)),
                   pl.BlockSpec((tm // LANES, khn.shape[0], LANES), lambda i: (i, 0, 0))],
        out_shape=[jax.ShapeDtypeStruct((t, d), BF16),
                   jax.ShapeDtypeStruct((knh.shape[0], t), F32),
                   jax.ShapeDtypeStruct((t // LANES, khn.shape[0], LANES), F32)],
        compiler_params=_cparams("parallel"),
        name="peer_scores",
    )(h2, g.reshape(1, d), wq_bf, knh, khn)


def _oddeven_merge_sort_pairs(n):
    pairs = []

    def merge(lo, m, r):
        step = r * 2
        if step < m:
            merge(lo, m, step)
            merge(lo + r, m, step)
            for i in range(lo + r, lo + m - r, step):
                pairs.append((i, i + r))
        else:
            pairs.append((lo, lo + r))

    def sort(lo, m):
        if m > 1:
            h = m // 2
            sort(lo, h)
            sort(lo + h, h)
            merge(lo, m, 1)

    sort(0, n)
    return pairs


_SORT16 = _oddeven_merge_sort_pairs(PEER_TOPK)


def _sort_desc(vals):
    vals = list(vals)
    for i, j in _SORT16:
        a, b = vals[i], vals[j]
        vals[i], vals[j] = jnp.maximum(a, b), jnp.minimum(a, b)
    return vals


def _merge_top(a, b):
    n = len(a)
    c = [jnp.maximum(a[i], b[n - 1 - i]) for i in range(n)]
    stride = n // 2
    while stride >= 1:
        for i in range(n):
            if (i // stride) % 2 == 0:
                x, y = c[i], c[i + stride]
                c[i], c[i + stride] = jnp.maximum(x, y), jnp.minimum(x, y)
        stride //= 2
    return c


def _top_sorted(vals):
    k = PEER_TOPK
    groups = [_sort_desc(vals[i:i + k]) for i in range(0, len(vals), k)]
    while len(groups) > 1:
        groups = [_merge_top(groups[i], groups[i + 1]) for i in range(0, len(groups), 2)]
    return groups[0]


def _peer_select_kernel(scnh_ref, s2_ref, c1_ref, g1_ref, e2_ref, *, nchunk):
    nk, nh, k = PEER_N_KEYS, PEER_HEADS, PEER_TOPK
    big = -NEG_INF
    for lc in range(nchunk):
        ls = slice(lc * LANES, (lc + 1) * LANES)
        x1 = [scnh_ref[n * nh:(n + 1) * nh, ls] for n in range(nk)]
        x2 = [scnh_ref[(nk + n) * nh:(nk + n + 1) * nh, ls] for n in range(nk)]
        a = _top_sorted(x1)
        b = _top_sorted(x2)
        cand = [a[i] + b[j] for i in range(k) for j in range(k) if (i + 1) * (j + 1) <= k]
        pad = [jnp.full_like(a[0], NEG_INF)] * (-len(cand) % k)
        top = _top_sorted(cand + pad)
        c16 = top[k - 1]
        c17 = functools.reduce(jnp.maximum, [jnp.where(c < c16, c, NEG_INF) for c in cand])
        thr = 0.5 * (c16 + c17)
        mx = a[0] + b[0]
        z = functools.reduce(lambda u, v: u + v, [jnp.where(c > thr, jnp.exp(c - mx), 0.0) for c in cand])
        inv_z = 1.0 / z
        for n in range(nk):
            keep = x1[n] >= a[k - 1]
            c1_ref[lc, n * nh:(n + 1) * nh, :] = jnp.where(keep, thr - x1[n], big)
            g1_ref[lc, n * nh:(n + 1) * nh, :] = jnp.exp(x1[n] - a[0])
        for h in range(nh):
            s2 = s2_ref[lc, h * nk:(h + 1) * nk, :]
            e = jnp.exp(s2 - b[0][h:h + 1, :]) * inv_z[h:h + 1, :]
            e2_ref[lc, h * nk:(h + 1) * nk, :] = jnp.where(s2 >= b[k - 1][h:h + 1, :], e, 0.0)


def _peer_select(scnh, s2):
    nt, rows, _ = s2.shape
    nchunk = min(nt, 2)
    spec = pl.BlockSpec((nchunk, rows, LANES), lambda i: (i, 0, 0))
    return pl.pallas_call(
        functools.partial(_peer_select_kernel, nchunk=nchunk),
        grid=(nt // nchunk,),
        in_specs=[pl.BlockSpec((scnh.shape[0], nchunk * LANES), lambda i: (0, i)), spec],
        out_specs=[spec, spec, spec],
        out_shape=[jax.ShapeDtypeStruct(s2.shape, F32)] * 3,
        compiler_params=_cparams("parallel"),
        name="peer_select",
    )(scnh, s2)


def _gelu(x):
    return 0.5 * x * (1.0 + lax.erf(x * (2.0 ** -0.5)))


def _peer_mix_kernel(xt_ref, u_ref, vt_ref, s2_ref, e2_ref, c1_ref, g1_ref, out_ref, hid_ref, a_ref, *, te, tm, cw):
    nk, nh = PEER_N_KEYS, PEER_HEADS
    nslab = te // nk
    jrows = nk // 2
    tiles = cw // LANES

    @pl.when(pl.program_id(1) == 0)
    def _():
        out_ref[...] = jnp.zeros_like(out_ref)
        hid_ref[...] = jnp.zeros_like(hid_ref)

    for c in range(tm // cw):
        for l in range(tiles):
            lt = c * tiles + l
            for jh in range(2):
                w = [jnp.zeros((jrows, LANES), F32) for _ in range(nslab)]
                for h in range(nh):
                    r0 = h * nk + jh * jrows
                    s2 = s2_ref[lt, r0:r0 + jrows, :]
                    e2 = e2_ref[lt, r0:r0 + jrows, :]
                    for ii in range(nslab):
                        r = ii * nh + h
                        sel = s2 >= c1_ref[lt, r:r + 1, :]
                        w[ii] = w[ii] + jnp.where(sel, e2 * g1_ref[lt, r:r + 1, :], 0.0)
                for ii in range(nslab):
                    r0 = ii * nk + jh * jrows
                    a = _gelu(hid_ref[lt, r0:r0 + jrows, :]) * w[ii]
                    a_ref[r0:r0 + jrows, lt * LANES:(lt + 1) * LANES] = a.astype(BF16)
        out_ref[:, c * cw:(c + 1) * cw] += _dot(vt_ref[...], a_ref[:, c * cw:(c + 1) * cw])
        hid = _dot(u_ref[...], xt_ref[:, c * cw:(c + 1) * cw])
        for l in range(tiles):
            hid_ref[c * tiles + l] = hid[:, l * LANES:(l + 1) * LANES]


def _peer_mix(xt, u_bf, vt_bf, s2, e2, c1, g1):
    d, t = xt.shape
    n_exp = u_bf.shape[0]
    nt, rows, _ = s2.shape
    tm = min(t, 1024)
    te = 512
    cw = min(tm, 256)
    n = n_exp // te
    crow = te // PEER_N_KEYS * PEER_HEADS
    once = pl.Buffered(1)
    tok = pl.BlockSpec((tm // LANES, rows, LANES), lambda i, s: (i, 0, 0), pipeline_mode=once)
    sel = pl.BlockSpec((tm // LANES, crow, LANES), lambda i, s: (i, jnp.maximum(s - 1, 0), 0))
    return pl.pallas_call(
        functools.partial(_peer_mix_kernel, te=te, tm=tm, cw=cw),
        grid=(t // tm, n + 1),
        in_specs=[pl.BlockSpec((d, tm), lambda i, s: (0, i), pipeline_mode=once),
                  pl.BlockSpec((te, d), lambda i, s: (jnp.minimum(s, n - 1), 0)),
                  pl.BlockSpec((d, te), lambda i, s: (0, jnp.maximum(s - 1, 0))),
                  tok, tok, sel, sel],
        out_specs=pl.BlockSpec((d, tm), lambda i, s: (0, i)),
        out_shape=jax.ShapeDtypeStruct((d, t), F32),
        scratch_shapes=[pltpu.VMEM((tm // LANES, te, LANES), F32), pltpu.VMEM((te, tm), BF16)],
        compiler_params=_cparams("parallel", "arbitrary"),
        name="peer_mix",
    )(xt, u_bf, vt_bf, s2, e2, c1, g1)


def _ple_kernel(h_ref, po_ref, g_ref, wg_ref, p_ref, wp_ref, fn_ref, out_ref, *, final):
    h = h_ref[...] + po_ref[...]
    hn = _rms(h, g_ref[...]).astype(BF16)
    gate = jax.nn.sigmoid(_dot(hn, wg_ref[...]))
    emb = _dot(p_ref[...].astype(BF16), wp_ref[...])
    y = h + emb * gate
    if final:
        y = _rms(y, fn_ref[...])
    out_ref[...] = y


def _ple(h2, peer_out, g, wg_bf, p2, wp_bf, fn, final):
    t, d = h2.shape
    tm = min(t, 256)
    rd = pl.BlockSpec((tm, d), lambda i: (i, 0))
    one = pl.BlockSpec((1, d), lambda i: (0, 0))
    full = lambda a: pl.BlockSpec(a.shape, lambda i: (0, 0))
    return pl.pallas_call(
        functools.partial(_ple_kernel, final=final),
        grid=(t // tm,),
        in_specs=[rd, rd, one, full(wg_bf), pl.BlockSpec((tm, p2.shape[1]), lambda i: (i, 0)), full(wp_bf), one],
        out_specs=rd,
        out_shape=jax.ShapeDtypeStruct((t, d), F32),
        compiler_params=_cparams("parallel"),
        name="ple_final" if final else "ple",
    )(h2, peer_out, g.reshape(1, d), wg_bf, p2, wp_bf, fn.reshape(1, d))


def _subkey_matrices(sub):
    nh, _, nk, kd = sub.shape
    eye_h = jnp.eye(nh, dtype=sub.dtype)
    eye_c = jnp.eye(2, dtype=sub.dtype)
    knh = jnp.einsum("hcnd,hg,ce->cnhged", sub, eye_h, eye_c).reshape(2 * nk * nh, nh * 2 * kd)
    khn = jnp.einsum("hnd,hg,e->hnged", sub[:, 1], eye_h, eye_c[1]).reshape(nh * nk, nh * 2 * kd)
    return knh.astype(BF16), khn.astype(BF16)


def kernel(x, p, positions, attn_norm, w_in, out_norm_a, out_norm_b, w_out, ffn_norm, peer_wq, peer_subkeys,
           peer_u, peer_v, ple_norm, ple_gate, ple_proj, final_norm):
    b, s, d = x.shape
    depth = w_in.shape[0]
    t = b * s
    assert s % (LANES * max(dil for _, dil in DILATED_PATTERNS)) == 0
    assert all(win // dil == HEAD_DIM for win, dil in DILATED_PATTERNS)
    cos, sin = _rope_tables(positions)
    h = x.reshape(t, d)
    for i in range(depth):
        pa, pb = _in_proj(h, attn_norm[i], w_in[i].astype(BF16), cos, sin)
        oa = _dilated_attn(pa, b, s)
        ob = _sb_attn(pb, b, s)
        h = _mix_out(oa, ob, out_norm_a[i], out_norm_b[i], w_out[i].astype(BF16), h)
        knh, khn = _subkey_matrices(peer_subkeys[i])
        xn, scnh, s2 = _peer_scores(h, ffn_norm[i], peer_wq[i].astype(BF16), knh, khn)
        c1, g1, e2 = _peer_select(scnh, s2)
        peer_t = _peer_mix(xn.T, peer_u[i].astype(BF16), peer_v[i].T.astype(BF16), s2, e2, c1, g1)
        h = _ple(h, peer_t.T, ple_norm[i], ple_gate[i].astype(BF16), p[i].reshape(t, -1),
                 ple_proj[i].astype(BF16), final_norm, final=(i == depth - 1))
    return h.reshape(b, s, d)
```

```python
import functools
import math

import jax
import jax.numpy as jnp
from jax import lax
from jax.experimental import pallas as pl
from jax.experimental.pallas import tpu as pltpu

F32 = jnp.float32
BF16 = jnp.bfloat16

HEAD_DIM = 128
N_HEADS_A = 8
N_HEADS_B = 8
WIDTH_A = N_HEADS_A * HEAD_DIM
WIDTH_B = N_HEADS_B * HEAD_DIM
DILATED_PATTERNS = ((128, 1), (512, 4), (2048, 16))
ROPE_THETA = 10000.0
PEER_HEADS = 8
PEER_N_KEYS = 128
PEER_TOPK = 16
NORM_EPS = 1e-6
NEG_INF = -1e30

LANES = 128
SUBLANES = 8
VMEM_LIMIT = 56 * 1024 * 1024
F32_EXP_ZERO = -104.0
LOG2E = math.log2(math.e)


def _cparams(*sem):
    return pltpu.CompilerParams(dimension_semantics=sem, vmem_limit_bytes=VMEM_LIMIT)


def _dot(a, b):
    return jnp.dot(a, b, preferred_element_type=F32)


def _dot_nt(a, b):
    return lax.dot_general(a, b, (((1,), (1,)), ((), ())), preferred_element_type=F32)


def _rms(x, g):
    return x * lax.rsqrt(jnp.mean(x * x, axis=-1, keepdims=True) + NORM_EPS) * g


def _rope_kernel(pos_ref, freq_ref, sign_ref, cos_ref, sin_ref):
    ang = pos_ref[...].astype(F32) * freq_ref[...]
    cos_ref[...] = jnp.cos(ang)
    sin_ref[...] = jnp.sin(ang) * sign_ref[...]


def _rope_tables(positions):
    t = positions.size
    half = HEAD_DIM // 2
    inv_freq = ROPE_THETA ** (-jnp.arange(half, dtype=F32) / half)
    freq = jnp.concatenate([inv_freq, inv_freq]).reshape(1, HEAD_DIM)
    sign = jnp.concatenate([-jnp.ones((half,), F32), jnp.ones((half,), F32)]).reshape(1, HEAD_DIM)
    tm = min(t, 1024)
    row = pl.BlockSpec((tm, HEAD_DIM), lambda i: (i, 0))
    one = pl.BlockSpec((1, HEAD_DIM), lambda i: (0, 0))
    return pl.pallas_call(
        _rope_kernel,
        grid=(t // tm,),
        in_specs=[pl.BlockSpec((tm, 1), lambda i: (i, 0)), one, one],
        out_specs=[row, row],
        out_shape=[jax.ShapeDtypeStruct((t, HEAD_DIM), F32)] * 2,
        compiler_params=_cparams("parallel"),
        name="rope_tables",
    )(positions.reshape(t, 1), freq, sign)


def _in_proj_kernel(x_ref, g_ref, w_ref, cos_ref, sin_ref, pa_ref, pb_ref, xn_ref, *, scale):
    j = pl.program_id(1)

    @pl.when(j == 0)
    def _():
        xn_ref[...] = _rms(x_ref[...], g_ref[...]).astype(BF16)

    acc = _dot(xn_ref[...], w_ref[...])

    @pl.when(j < 2)
    def _():
        sc = jnp.where(j == 0, scale, 1.0).astype(F32)
        c = cos_ref[...] * sc
        s = sin_ref[...] * sc
        for hh in range(N_HEADS_A):
            cs = slice(hh * HEAD_DIM, (hh + 1) * HEAD_DIM)
            a = acc[:, cs]
            pa_ref[:, cs] = a * c + pltpu.roll(a, HEAD_DIM // 2, 1) * s

    @pl.when(j == 2)
    def _():
        pa_ref[...] = acc

    @pl.when(j >= 3)
    def _():
        sc = jnp.where(j == 3, scale * LOG2E, 1.0).astype(F32)
        pb_ref[...] = (acc * sc).astype(BF16)


def _in_proj(h2, g, w_bf, cos, sin):
    t, d = h2.shape
    n = w_bf.shape[1]
    tm = min(t, 512)
    tn = WIDTH_A
    na = 3
    return pl.pallas_call(
        functools.partial(_in_proj_kernel, scale=HEAD_DIM ** -0.5),
        grid=(t // tm, n // tn),
        in_specs=[
            pl.BlockSpec((tm, d), lambda i, j: (i, 0)),
            pl.BlockSpec((1, d), lambda i, j: (0, 0)),
            pl.BlockSpec((d, tn), lambda i, j: (0, j)),
            pl.BlockSpec((tm, HEAD_DIM), lambda i, j: (i, 0)),
            pl.BlockSpec((tm, HEAD_DIM), lambda i, j: (i, 0)),
        ],
        out_specs=[pl.BlockSpec((tm, tn), lambda i, j: (i, jnp.minimum(j, na - 1))),
                   pl.BlockSpec((tm, tn), lambda i, j: (i, jnp.maximum(j - na, 0)))],
        out_shape=[jax.ShapeDtypeStruct((t, na * tn), F32), jax.ShapeDtypeStruct((t, n - na * tn), BF16)],
        scratch_shapes=[pltpu.VMEM((tm, d), BF16)],
        compiler_params=_cparams("parallel", "arbitrary"),
        name="in_proj",
    )(h2, g.reshape(1, d), w_bf, cos, sin)


def _dilated_kernel(q_ref, k_ref, v_ref, o_ref, oacc, lacc, *, seq):
    w = HEAD_DIM
    diff = (lax.broadcasted_iota(jnp.int32, (w, 2 * w), 0) - lax.broadcasted_iota(jnp.int32, (w, 2 * w), 1))
    ones = jnp.ones((2 * w, w), BF16)

    def block(blk, r, d, first):
        span = w * d
        pblk = jnp.maximum(blk - 1, 0)
        qwin = pl.ds(pl.multiple_of(blk * span, span), span)
        kwin = pl.ds(pl.multiple_of(pblk * span, span), 2 * span)
        rows = (lambda n: pl.ds(r, n, stride=d)) if d > 1 else (lambda n: pl.ds(0, n))
        valid = jnp.abs(diff + ((blk - pblk) * w - w // 2)) <= w // 2
        q = q_ref.at[qwin][rows(w), :].astype(BF16)
        k2 = k_ref.at[kwin][rows(2 * w), :].astype(BF16)
        v2 = v_ref.at[kwin][rows(2 * w), :].astype(BF16)
        s = jnp.where(valid, _dot_nt(q, k2), NEG_INF)
        m = jnp.max(s, axis=-1, keepdims=True)
        e = jnp.exp(s - m).astype(BF16)
        pv = _dot(e, jnp.concatenate([v2, ones], axis=1))
        l = pv[:, w:]
        o = pv[:, :w] / l
        lse = m + jnp.log(l)
        if first:
            oacc.at[qwin][rows(w), :] = o
            lacc.at[qwin][rows(w), :] = lse
        else:
            o0 = oacc.at[qwin][rows(w), :]
            l0 = lacc.at[qwin][rows(w), :]
            mx = jnp.maximum(l0, lse)
            e0 = jnp.exp(l0 - mx)
            e1 = jnp.exp(lse - mx)
            den = e0 + e1
            oacc.at[qwin][rows(w), :] = (e0 * o0 + e1 * o) / den
            lacc.at[qwin][rows(w), :] = mx + jnp.log(den)

    for pi, (_, d) in enumerate(DILATED_PATTERNS):
        nblk = seq // d // w
        ub = max(1, 4 // d)

        def body(g, carry, d=d, ub=ub, first=(pi == 0)):
            for u in range(ub):
                for r in range(d):
                    block(g * ub + u, r, d, first)
            return carry

        lax.fori_loop(0, nblk // ub, body, 0)
    o_ref[...] = oacc[...].astype(BF16)


def _dilated_attn(pa, b, s):
    view = pa.reshape(b, s, pa.shape[1])

    def in_map(x):
        return lambda bi, h: (bi, 0, x * N_HEADS_A + h)

    spec = lambda x: pl.BlockSpec((None, s, HEAD_DIM), in_map(x))
    o = pl.pallas_call(
        functools.partial(_dilated_kernel, seq=s),
        grid=(b, N_HEADS_A),
        in_specs=[spec(0), spec(1), spec(2)],
        out_specs=pl.BlockSpec((None, s, HEAD_DIM), lambda bi, h: (bi, 0, h)),
        out_shape=jax.ShapeDtypeStruct((b, s, WIDTH_A), BF16),
        scratch_shapes=[pltpu.VMEM((s, HEAD_DIM), F32), pltpu.VMEM((s, HEAD_DIM), F32)],
        compiler_params=_cparams("parallel", "parallel"),
        name="dilated_attn",
    )(view, view, view)
    return o.reshape(b * s, WIDTH_A)


def _sb_kernel(q_ref, k_ref, v_ref, o_ref, *, blk, nq, hg):
    row = lax.broadcasted_iota(jnp.int32, (blk, blk), 0)
    col = lax.broadcasted_iota(jnp.int32, (blk, blk), 1)
    tri = jnp.where(row > col, 1.0, 0.0).astype(BF16)
    causal = col < row
    heads = [slice(h * HEAD_DIM, (h + 1) * HEAD_DIM) for h in range(hg)]

    def qblock(i, carry):
        q0 = pl.multiple_of(i * blk, blk)
        qs = [q_ref[pl.ds(q0, blk), cs] for cs in heads]

        def step(st, diagonal):
            kb, cs_, accs = st
            k0 = pl.multiple_of(kb * blk, blk)
            new_c, new_acc = [], []
            for q, c, acc, cs in zip(qs, cs_, accs, heads):
                z = _dot_nt(q, k_ref[pl.ds(k0, blk), cs])
                sp = jnp.maximum(z, 0.0) + jnp.log2(1.0 + jnp.exp2(-jnp.abs(z)))
                lm = jnp.where(causal, -sp, 0.0) if diagonal else -sp
                hi = lm.astype(BF16)
                lo = (lm - hi.astype(F32)).astype(BF16)
                later = _dot(hi, tri) + _dot(lo, tri)
                a = jnp.exp2(z - sp + later + c)
                if diagonal:
                    a = jnp.where(causal, a, 0.0)
                new_acc.append(acc + _dot(a.astype(BF16), v_ref[pl.ds(k0, blk), cs]))
                new_c.append(c + jnp.sum(lm, axis=-1, keepdims=True))
            return kb - 1, tuple(new_c), tuple(new_acc)

        def cond(st):
            kb, cs_, _ = st
            live = jnp.max(functools.reduce(jnp.maximum, cs_)) > F32_EXP_ZERO * LOG2E
            return jnp.logical_and(kb >= 0, live)

        init = (i, tuple(jnp.zeros((blk, 1), F32) for _ in heads),
                tuple(jnp.zeros((blk, HEAD_DIM), F32) for _ in heads))
        _, _, accs = lax.while_loop(cond, lambda st: step(st, False), step(init, True))
        for acc, cs in zip(accs, heads):
            o_ref[pl.ds(q0, blk), cs] = acc.astype(BF16)
        return carry

    lax.fori_loop(0, nq, qblock, 0)


def _sb_attn(pb, b, s):
    view = pb.reshape(b, s, pb.shape[1])
    blk = min(s, 256)
    hg = 4
    cw = hg * HEAD_DIM
    ngrp = N_HEADS_B // hg

    def in_map(x):
        return lambda bi, g: (bi, 0, x * ngrp + g)

    o = pl.pallas_call(
        functools.partial(_sb_kernel, blk=blk, nq=s // blk, hg=hg),
        grid=(b, ngrp),
        in_specs=[pl.BlockSpec((None, s, cw), in_map(0)),
                  pl.BlockSpec((None, s, cw), in_map(1)),
                  pl.BlockSpec((None, s, cw), in_map(2))],
        out_specs=pl.BlockSpec((None, s, cw), lambda bi, g: (bi, 0, g)),
        out_shape=jax.ShapeDtypeStruct((b, s, WIDTH_B), BF16),
        compiler_params=_cparams("parallel", "parallel"),
        name="stick_breaking",
    )(view, view, view)
    return o.reshape(b * s, WIDTH_B)


def _mix_out_kernel(oa_ref, ob_ref, ga_ref, gb_ref, w_ref, h_ref, out_ref):
    na = _rms(oa_ref[...].astype(F32), ga_ref[...]).astype(BF16)
    nb = _rms(ob_ref[...].astype(F32), gb_ref[...]).astype(BF16)
    y = _dot(na, w_ref[:WIDTH_A, :]) + _dot(nb, w_ref[WIDTH_A:, :])
    out_ref[...] = h_ref[...] + y


def _mix_out(oa, ob, ga, gb, w_bf, h2):
    t, d = h2.shape
    tm = min(t, 512)
    ra = pl.BlockSpec((tm, WIDTH_A), lambda i: (i, 0))
    rd = pl.BlockSpec((tm, d), lambda i: (i, 0))
    return pl.pallas_call(
        _mix_out_kernel,
        grid=(t // tm,),
        in_specs=[ra, ra,
                  pl.BlockSpec((1, WIDTH_A), lambda i: (0, 0)),
                  pl.BlockSpec((1, WIDTH_B), lambda i: (0, 0)),
                  pl.BlockSpec(w_bf.shape, lambda i: (0, 0)),
                  rd],
        out_specs=rd,
        out_shape=jax.ShapeDtypeStruct((t, d), F32),
        compiler_params=_cparams("parallel"),
        name="mix_out_proj",
    )(oa, ob, ga.reshape(1, -1), gb.reshape(1, -1), w_bf, h2)


def _peer_scores_kernel(h_ref, g_ref, wq_ref, sub_ref, xt_ref, scnh_ref, s2_ref):
    nh, nk = PEER_HEADS, PEER_N_KEYS
    xn = _rms(h_ref[...], g_ref[...])
    xt_ref[...] = xn.T.astype(BF16)
    qr = _dot(xn.astype(BF16), wq_ref[...]).astype(BF16)
    for h in range(nh):
        for c in range(2):
            col = (h * 2 + c) * nk
            sc = _dot_nt(sub_ref[h, c], qr[:, col:col + nk])
            for lt in range(s2_ref.shape[0]):
                sl = sc[:, lt * LANES:(lt + 1) * LANES]
                scnh_ref[lt, pl.ds(c * nk * nh + h, nk, stride=nh), :] = sl
                if c == 1:
                    s2_ref[lt, h * nk:(h + 1) * nk, :] = sl


def _peer_scores(h2, g, wq_bf, sub_bf):
    t, d = h2.shape
    nh, _, nk, _ = sub_bf.shape
    tm = min(t, 512)
    return pl.pallas_call(
        _peer_scores_kernel,
        grid=(t // tm,),
        in_specs=[pl.BlockSpec((tm, d), lambda i: (i, 0)),
                  pl.BlockSpec((1, d), lambda i: (0, 0)),
                  pl.BlockSpec(wq_bf.shape, lambda i: (0, 0)),
                  pl.BlockSpec(sub_bf.shape, lambda i: (0, 0, 0, 0))],
        out_specs=[pl.BlockSpec((d, tm), lambda i: (0, i)),
                   pl.BlockSpec((tm // LANES, 2 * nk * nh, LANES), lambda i: (i, 0, 0)),
                   pl.BlockSpec((tm // LANES, nh * nk, LANES), lambda i: (i, 0, 0))],
        out_shape=[jax.ShapeDtypeStruct((d, t), BF16),
                   jax.ShapeDtypeStruct((t // LANES, 2 * nk * nh, LANES), F32),
                   jax.ShapeDtypeStruct((t // LANES, nh * nk, LANES), F32)],
        compiler_params=_cparams("parallel"),
        name="peer_scores",
    )(h2, g.reshape(1, d), wq_bf, sub_bf)


def _oddeven_merge_sort_pairs(n):
    pairs = []

    def merge(lo, m, r):
        step = r * 2
        if step < m:
            merge(lo, m, step)
            merge(lo + r, m, step)
            for i in range(lo + r, lo + m - r, step):
                pairs.append((i, i + r))
        else:
            pairs.append((lo, lo + r))

    def sort(lo, m):
        if m > 1:
            h = m // 2
            sort(lo, h)
            sort(lo + h, h)
            merge(lo, m, 1)

    sort(0, n)
    return pairs


_SORT16 = _oddeven_merge_sort_pairs(PEER_TOPK)


def _sort_desc(vals):
    vals = list(vals)
    for i, j in _SORT16:
        a, b = vals[i], vals[j]
        vals[i], vals[j] = jnp.maximum(a, b), jnp.minimum(a, b)
    return vals


def _merge_top(a, b):
    n = len(a)
    c = [jnp.maximum(a[i], b[n - 1 - i]) for i in range(n)]
    stride = n // 2
    while stride >= 1:
        for i in range(n):
            if (i // stride) % 2 == 0:
                x, y = c[i], c[i + stride]
                c[i], c[i + stride] = jnp.maximum(x, y), jnp.minimum(x, y)
        stride //= 2
    return c


def _top_sorted(vals):
    k = PEER_TOPK
    groups = [_sort_desc(vals[i:i + k]) for i in range(0, len(vals), k)]
    while len(groups) > 1:
        groups = [_merge_top(groups[i], groups[i + 1]) for i in range(0, len(groups), 2)]
    return groups[0]


def _peer_select_kernel(scnh_ref, s2_ref, c1_ref, g1_ref, e2_ref, *, nchunk):
    nk, nh, k = PEER_N_KEYS, PEER_HEADS, PEER_TOPK
    big = -NEG_INF
    for lc in range(nchunk):
        x1 = [scnh_ref[lc, n * nh:(n + 1) * nh, :] for n in range(nk)]
        x2 = [scnh_ref[lc, (nk + n) * nh:(nk + n + 1) * nh, :] for n in range(nk)]
        a = _top_sorted(x1)
        b = _top_sorted(x2)
        cand = [a[i] + b[j] for i in range(k) for j in range(k) if (i + 1) * (j + 1) <= k]
        pad = [jnp.full_like(a[0], NEG_INF)] * (-len(cand) % k)
        top = _top_sorted(cand + pad)
        c16 = top[k - 1]
        c17 = functools.reduce(jnp.maximum, [jnp.where(c < c16, c, NEG_INF) for c in cand])
        thr = 0.5 * (c16 + c17)
        mx = a[0] + b[0]
        z = functools.reduce(lambda u, v: u + v, [jnp.where(c > thr, jnp.exp(c - mx), 0.0) for c in cand])
        inv_z = 1.0 / z
        for n in range(nk):
            keep = x1[n] >= a[k - 1]
            c1_ref[lc, n * nh:(n + 1) * nh, :] = jnp.where(keep, thr - x1[n], big)
            g1_ref[lc, n * nh:(n + 1) * nh, :] = jnp.exp(x1[n] - a[0])
        for h in range(nh):
            s2 = s2_ref[lc, h * nk:(h + 1) * nk, :]
            e = jnp.exp(s2 - b[0][h:h + 1, :]) * inv_z[h:h + 1, :]
            e2_ref[lc, h * nk:(h + 1) * nk, :] = jnp.where(s2 >= b[k - 1][h:h + 1, :], e, 0.0)


def _peer_select(scnh, s2):
    nt, rows, _ = s2.shape
    nchunk = min(nt, 2)
    spec = pl.BlockSpec((nchunk, rows, LANES), lambda i: (i, 0, 0))
    return pl.pallas_call(
        functools.partial(_peer_select_kernel, nchunk=nchunk),
        grid=(nt // nchunk,),
        in_specs=[pl.BlockSpec((nchunk, scnh.shape[1], LANES), lambda i: (i, 0, 0)), spec],
        out_specs=[spec, spec, spec],
        out_shape=[jax.ShapeDtypeStruct(s2.shape, F32)] * 3,
        compiler_params=_cparams("parallel"),
        name="peer_select",
    )(scnh, s2)


def _gelu(x):
    return 0.5 * x * (1.0 + lax.erf(x * (2.0 ** -0.5)))


def _peer_mix_kernel(xt_ref, u_ref, vt_ref, s2_ref, e2_ref, c1_ref, g1_ref, out_ref, hid_ref, a_ref, *, te, tm, cw):
    nk, nh = PEER_N_KEYS, PEER_HEADS
    nslab = te // nk
    jrows = nk // 2
    tiles = cw // LANES

    @pl.when(pl.program_id(1) == 0)
    def _():
        out_ref[...] = jnp.zeros_like(out_ref)
        hid_ref[...] = jnp.zeros_like(hid_ref)

    for c in range(tm // cw):
        for l in range(tiles):
            lt = c * tiles + l
            for jh in range(2):
                w = [jnp.zeros((jrows, LANES), F32) for _ in range(nslab)]
                for h in range(nh):
                    r0 = h * nk + jh * jrows
                    s2 = s2_ref[lt, r0:r0 + jrows, :]
                    e2 = e2_ref[lt, r0:r0 + jrows, :]
                    for ii in range(nslab):
                        r = ii * nh + h
                        sel = s2 >= c1_ref[lt, r:r + 1, :]
                        w[ii] = w[ii] + jnp.where(sel, e2 * g1_ref[lt, r:r + 1, :], 0.0)
                for ii in range(nslab):
                    r0 = ii * nk + jh * jrows
                    a = _gelu(hid_ref[lt, r0:r0 + jrows, :]) * w[ii]
                    a_ref[r0:r0 + jrows, lt * LANES:(lt + 1) * LANES] = a.astype(BF16)
        out_ref[:, c * cw:(c + 1) * cw] += _dot(vt_ref[...], a_ref[:, c * cw:(c + 1) * cw])
        hid = _dot(u_ref[...], xt_ref[:, c * cw:(c + 1) * cw])
        for l in range(tiles):
            hid_ref[c * tiles + l] = hid[:, l * LANES:(l + 1) * LANES]


def _peer_mix(xt, u_bf, vt_bf, s2, e2, c1, g1):
    d, t = xt.shape
    n_exp = u_bf.shape[0]
    nt, rows, _ = s2.shape
    tm = min(t, 1024)
    te = 512
    cw = min(tm, 256)
    n = n_exp // te
    crow = te // PEER_N_KEYS * PEER_HEADS
    once = pl.Buffered(1)
    tok = pl.BlockSpec((tm // LANES, rows, LANES), lambda i, s: (i, 0, 0), pipeline_mode=once)
    sel = pl.BlockSpec((tm // LANES, crow, LANES), lambda i, s: (i, jnp.maximum(s - 1, 0), 0))
    return pl.pallas_call(
        functools.partial(_peer_mix_kernel, te=te, tm=tm, cw=cw),
        grid=(t // tm, n + 1),
        in_specs=[pl.BlockSpec((d, tm), lambda i, s: (0, i), pipeline_mode=once),
                  pl.BlockSpec((te, d), lambda i, s: (jnp.minimum(s, n - 1), 0)),
                  pl.BlockSpec((d, te), lambda i, s: (0, jnp.maximum(s - 1, 0))),
                  tok, tok, sel, sel],
        out_specs=pl.BlockSpec((d, tm), lambda i, s: (0, i)),
        out_shape=jax.ShapeDtypeStruct((d, t), F32),
        scratch_shapes=[pltpu.VMEM((tm // LANES, te, LANES), F32), pltpu.VMEM((te, tm), BF16)],
        compiler_params=_cparams("parallel", "arbitrary"),
        name="peer_mix",
    )(xt, u_bf, vt_bf, s2, e2, c1, g1)


def _ple_kernel(h_ref, pt_ref, g_ref, wg_ref, p_ref, wp_ref, fn_ref, out_ref, *, final):
    h = h_ref[...] + pt_ref[...].T
    hn = _rms(h, g_ref[...]).astype(BF16)
    gate = jax.nn.sigmoid(_dot(hn, wg_ref[...]))
    emb = _dot(p_ref[...].astype(BF16), wp_ref[...])
    y = h + emb * gate
    if final:
        y = _rms(y, fn_ref[...])
    out_ref[...] = y


def _ple(h2, peer_t, g, wg_bf, p2, wp_bf, fn, final):
    t, d = h2.shape
    tm = min(t, 256)
    rd = pl.BlockSpec((tm, d), lambda i: (i, 0))
    one = pl.BlockSpec((1, d), lambda i: (0, 0))
    full = lambda a: pl.BlockSpec(a.shape, lambda i: (0, 0))
    return pl.pallas_call(
        functools.partial(_ple_kernel, final=final),
        grid=(t // tm,),
        in_specs=[rd, pl.BlockSpec((d, tm), lambda i: (0, i)), one, full(wg_bf),
                  pl.BlockSpec((tm, p2.shape[1]), lambda i: (i, 0)), full(wp_bf), one],
        out_specs=rd,
        out_shape=jax.ShapeDtypeStruct((t, d), F32),
        compiler_params=_cparams("parallel"),
        name="ple_final" if final else "ple",
    )(h2, peer_t, g.reshape(1, d), wg_bf, p2, wp_bf, fn.reshape(1, d))


def kernel(x, p, positions, attn_norm, w_in, out_norm_a, out_norm_b, w_out, ffn_norm, peer_wq, peer_subkeys,
           peer_u, peer_v, ple_norm, ple_gate, ple_proj, final_norm):
    b, s, d = x.shape
    depth = w_in.shape[0]
    t = b * s
    assert s % (LANES * max(dil for _, dil in DILATED_PATTERNS)) == 0
    assert all(win // dil == HEAD_DIM for win, dil in DILATED_PATTERNS)
    cos, sin = _rope_tables(positions)
    h = x.reshape(t, d)
    for i in range(depth):
        pa, pb = _in_proj(h, attn_norm[i], w_in[i].astype(BF16), cos, sin)
        oa = _dilated_attn(pa, b, s)
        ob = _sb_attn(pb, b, s)
        h = _mix_out(oa, ob, out_norm_a[i], out_norm_b[i], w_out[i].astype(BF16), h)
        xt, scnh, s2 = _peer_scores(h, ffn_norm[i], peer_wq[i].astype(BF16), peer_subkeys[i].astype(BF16))
        c1, g1, e2 = _peer_select(scnh, s2)
        peer_t = _peer_mix(xt, peer_u[i].astype(BF16), peer_v[i].T.astype(BF16), s2, e2, c1, g1)
        h = _ple(h, peer_t, ple_norm[i], ple_gate[i].astype(BF16), p[i].reshape(t, -1),
                 ple_proj[i].astype(BF16), final_norm, final=(i == depth - 1))
    return h.reshape(b, s, d)
```

```python
import functools
import math

import jax
import jax.numpy as jnp
from jax import lax
from jax.experimental import pallas as pl
from jax.experimental.pallas import tpu as pltpu

F32 = jnp.float32
BF16 = jnp.bfloat16

HEAD_DIM = 128
N_HEADS_A = 8
N_HEADS_B = 8
WIDTH_A = N_HEADS_A * HEAD_DIM
WIDTH_B = N_HEADS_B * HEAD_DIM
DILATED_PATTERNS = ((128, 1), (512, 4), (2048, 16))
ROPE_THETA = 10000.0
PEER_HEADS = 8
PEER_N_KEYS = 128
PEER_TOPK = 16
NORM_EPS = 1e-6
NEG_INF = -1e30

LANES = 128
SUBLANES = 8
VMEM_LIMIT = 56 * 1024 * 1024
F32_EXP_ZERO = -104.0
LOG2E = math.log2(math.e)


def _cparams(*sem):
    return pltpu.CompilerParams(dimension_semantics=sem, vmem_limit_bytes=VMEM_LIMIT)


def _dot(a, b):
    return jnp.dot(a, b, preferred_element_type=F32)


def _dot_nt(a, b):
    return lax.dot_general(a, b, (((1,), (1,)), ((), ())), preferred_element_type=F32)


def _rms(x, g):
    return x * lax.rsqrt(jnp.mean(x * x, axis=-1, keepdims=True) + NORM_EPS) * g


def _rope_kernel(pos_ref, freq_ref, sign_ref, cos_ref, sin_ref):
    ang = pos_ref[...].astype(F32) * freq_ref[...]
    cos_ref[...] = jnp.cos(ang)
    sin_ref[...] = jnp.sin(ang) * sign_ref[...]


def _rope_tables(positions):
    t = positions.size
    half = HEAD_DIM // 2
    inv_freq = ROPE_THETA ** (-jnp.arange(half, dtype=F32) / half)
    freq = jnp.concatenate([inv_freq, inv_freq]).reshape(1, HEAD_DIM)
    sign = jnp.concatenate([-jnp.ones((half,), F32), jnp.ones((half,), F32)]).reshape(1, HEAD_DIM)
    tm = min(t, 1024)
    row = pl.BlockSpec((tm, HEAD_DIM), lambda i: (i, 0))
    one = pl.BlockSpec((1, HEAD_DIM), lambda i: (0, 0))
    return pl.pallas_call(
        _rope_kernel,
        grid=(t // tm,),
        in_specs=[pl.BlockSpec((tm, 1), lambda i: (i, 0)), one, one],
        out_specs=[row, row],
        out_shape=[jax.ShapeDtypeStruct((t, HEAD_DIM), F32)] * 2,
        compiler_params=_cparams("parallel"),
        name="rope_tables",
    )(positions.reshape(t, 1), freq, sign)


def _in_proj_kernel(x_ref, g_ref, w_ref, cos_ref, sin_ref, pa_ref, pb_ref, xn_ref, *, scale):
    j = pl.program_id(1)

    @pl.when(j == 0)
    def _():
        xn_ref[...] = _rms(x_ref[...], g_ref[...]).astype(BF16)

    acc = _dot(xn_ref[...], w_ref[...])

    @pl.when(j < 2)
    def _():
        sc = jnp.where(j == 0, scale, 1.0).astype(F32)
        c = cos_ref[...] * sc
        s = sin_ref[...] * sc
        for hh in range(N_HEADS_A):
            cs = slice(hh * HEAD_DIM, (hh + 1) * HEAD_DIM)
            a = acc[:, cs]
            pa_ref[:, cs] = a * c + pltpu.roll(a, HEAD_DIM // 2, 1) * s

    @pl.when(j == 2)
    def _():
        pa_ref[...] = acc

    @pl.when(j >= 3)
    def _():
        sc = jnp.where(j == 3, scale * LOG2E, 1.0).astype(F32)
        pb_ref[...] = (acc * sc).astype(BF16)


def _in_proj(h2, g, w_bf, cos, sin):
    t, d = h2.shape
    n = w_bf.shape[1]
    tm = min(t, 1024)
    tn = WIDTH_A
    na = 3
    return pl.pallas_call(
        functools.partial(_in_proj_kernel, scale=HEAD_DIM ** -0.5),
        grid=(t // tm, n // tn),
        in_specs=[
            pl.BlockSpec((tm, d), lambda i, j: (i, 0)),
            pl.BlockSpec((1, d), lambda i, j: (0, 0)),
            pl.BlockSpec((d, tn), lambda i, j: (0, j)),
            pl.BlockSpec((tm, HEAD_DIM), lambda i, j: (i, 0)),
            pl.BlockSpec((tm, HEAD_DIM), lambda i, j: (i, 0)),
        ],
        out_specs=[pl.BlockSpec((tm, tn), lambda i, j: (i, jnp.minimum(j, na - 1))),
                   pl.BlockSpec((tm, tn), lambda i, j: (i, jnp.maximum(j - na, 0)))],
        out_shape=[jax.ShapeDtypeStruct((t, na * tn), F32), jax.ShapeDtypeStruct((t, n - na * tn), BF16)],
        scratch_shapes=[pltpu.VMEM((tm, d), BF16)],
        compiler_params=_cparams("parallel", "arbitrary"),
        name="in_proj",
    )(h2, g.reshape(1, d), w_bf, cos, sin)


def _dilated_kernel(q_ref, k_ref, v_ref, o_ref, oacc, lacc, *, seq):
    w = HEAD_DIM
    diff = (lax.broadcasted_iota(jnp.int32, (w, 2 * w), 0) - lax.broadcasted_iota(jnp.int32, (w, 2 * w), 1))
    ones = jnp.ones((2 * w, w), BF16)

    def block(blk, r, d, first):
        span = w * d
        pblk = jnp.maximum(blk - 1, 0)
        qwin = pl.ds(pl.multiple_of(blk * span, span), span)
        kwin = pl.ds(pl.multiple_of(pblk * span, span), 2 * span)
        rows = (lambda n: pl.ds(r, n, stride=d)) if d > 1 else (lambda n: pl.ds(0, n))
        valid = jnp.abs(diff + ((blk - pblk) * w - w // 2)) <= w // 2
        q = q_ref.at[qwin][rows(w), :].astype(BF16)
        k2 = k_ref.at[kwin][rows(2 * w), :].astype(BF16)
        v2 = v_ref.at[kwin][rows(2 * w), :].astype(BF16)
        s = jnp.where(valid, _dot_nt(q, k2), NEG_INF)
        m = jnp.max(s, axis=-1, keepdims=True)
        e = jnp.exp(s - m).astype(BF16)
        pv = _dot(e, jnp.concatenate([v2, ones], axis=1))
        l = pv[:, w:]
        o = pv[:, :w] / l
        lse = m + jnp.log(l)
        if first:
            oacc.at[qwin][rows(w), :] = o
            lacc.at[qwin][rows(w), :] = lse
        else:
            o0 = oacc.at[qwin][rows(w), :]
            l0 = lacc.at[qwin][rows(w), :]
            mx = jnp.maximum(l0, lse)
            e0 = jnp.exp(l0 - mx)
            e1 = jnp.exp(lse - mx)
            den = e0 + e1
            oacc.at[qwin][rows(w), :] = (e0 * o0 + e1 * o) / den
            lacc.at[qwin][rows(w), :] = mx + jnp.log(den)

    for pi, (_, d) in enumerate(DILATED_PATTERNS):
        nblk = seq // d // w
        ub = max(1, 4 // d)

        def body(g, carry, d=d, ub=ub, first=(pi == 0)):
            for u in range(ub):
                for r in range(d):
                    block(g * ub + u, r, d, first)
            return carry

        lax.fori_loop(0, nblk // ub, body, 0)
    o_ref[...] = oacc[...].astype(BF16)


def _dilated_attn(pa, b, s):
    view = pa.reshape(b, s, pa.shape[1])

    def in_map(x):
        return lambda bi, h: (bi, 0, x * N_HEADS_A + h)

    spec = lambda x: pl.BlockSpec((None, s, HEAD_DIM), in_map(x))
    o = pl.pallas_call(
        functools.partial(_dilated_kernel, seq=s),
        grid=(b, N_HEADS_A),
        in_specs=[spec(0), spec(1), spec(2)],
        out_specs=pl.BlockSpec((None, s, HEAD_DIM), lambda bi, h: (bi, 0, h)),
        out_shape=jax.ShapeDtypeStruct((b, s, WIDTH_A), BF16),
        scratch_shapes=[pltpu.VMEM((s, HEAD_DIM), F32), pltpu.VMEM((s, HEAD_DIM), F32)],
        compiler_params=_cparams("parallel", "parallel"),
        name="dilated_attn",
    )(view, view, view)
    return o.reshape(b * s, WIDTH_A)


def _sb_kernel(q_ref, k_ref, v_ref, o_ref, *, blk, nq, hg):
    row = lax.broadcasted_iota(jnp.int32, (blk, blk), 0)
    col = lax.broadcasted_iota(jnp.int32, (blk, blk), 1)
    tri = jnp.where(row > col, 1.0, 0.0).astype(BF16)
    causal = col < row
    heads = [slice(h * HEAD_DIM, (h + 1) * HEAD_DIM) for h in range(hg)]

    def qblock(i, carry):
        q0 = pl.multiple_of(i * blk, blk)
        qs = [q_ref[pl.ds(q0, blk), cs] for cs in heads]

        def step(st, diagonal):
            kb, cs_, accs = st
            k0 = pl.multiple_of(kb * blk, blk)
            logits, parts, new_c, new_acc = [], [], [], []
            for q, c, cs in zip(qs, cs_, heads):
                z = _dot_nt(q, k_ref[pl.ds(k0, blk), cs])
                sp = jnp.maximum(z, 0.0) + jnp.log2(1.0 + jnp.exp2(-jnp.abs(z)))
                lm = jnp.where(causal, -sp, 0.0) if diagonal else -sp
                hi = lm.astype(BF16)
                parts += [hi, (lm - hi.astype(F32)).astype(BF16)]
                logits.append(z - sp + c)
                new_c.append(c + jnp.sum(lm, axis=-1, keepdims=True))
            later = _dot(jnp.concatenate(parts, axis=0), tri)
            for h, (lg, acc, cs) in enumerate(zip(logits, accs, heads)):
                a = jnp.exp2(lg + later[2 * h * blk:(2 * h + 1) * blk] + later[(2 * h + 1) * blk:(2 * h + 2) * blk])
                if diagonal:
                    a = jnp.where(causal, a, 0.0)
                new_acc.append(acc + _dot(a.astype(BF16), v_ref[pl.ds(k0, blk), cs]))
            return kb - 1, tuple(new_c), tuple(new_acc)

        def cond(st):
            kb, cs_, _ = st
            live = jnp.max(functools.reduce(jnp.maximum, cs_)) > F32_EXP_ZERO * LOG2E
            return jnp.logical_and(kb >= 0, live)

        init = (i, tuple(jnp.zeros((blk, 1), F32) for _ in heads),
                tuple(jnp.zeros((blk, HEAD_DIM), F32) for _ in heads))
        _, _, accs = lax.while_loop(cond, lambda st: step(st, False), step(init, True))
        for acc, cs in zip(accs, heads):
            o_ref[pl.ds(q0, blk), cs] = acc.astype(BF16)
        return carry

    lax.fori_loop(0, nq, qblock, 0)


def _sb_attn(pb, b, s):
    view = pb.reshape(b, s, pb.shape[1])
    blk = min(s, 256)
    hg = 4
    cw = hg * HEAD_DIM
    ngrp = N_HEADS_B // hg

    def in_map(x):
        return lambda bi, g: (bi, 0, x * ngrp + g)

    o = pl.pallas_call(
        functools.partial(_sb_kernel, blk=blk, nq=s // blk, hg=hg),
        grid=(b, ngrp),
        in_specs=[pl.BlockSpec((None, s, cw), in_map(0)),
                  pl.BlockSpec((None, s, cw), in_map(1)),
                  pl.BlockSpec((None, s, cw), in_map(2))],
        out_specs=pl.BlockSpec((None, s, cw), lambda bi, g: (bi, 0, g)),
        out_shape=jax.ShapeDtypeStruct((b, s, WIDTH_B), BF16),
        compiler_params=_cparams("parallel", "parallel"),
        name="stick_breaking",
    )(view, view, view)
    return o.reshape(b * s, WIDTH_B)


def _mix_out_kernel(oa_ref, ob_ref, ga_ref, gb_ref, w_ref, h_ref, out_ref):
    na = _rms(oa_ref[...].astype(F32), ga_ref[...]).astype(BF16)
    nb = _rms(ob_ref[...].astype(F32), gb_ref[...]).astype(BF16)
    y = _dot(na, w_ref[:WIDTH_A, :]) + _dot(nb, w_ref[WIDTH_A:, :])
    out_ref[...] = h_ref[...] + y


def _mix_out(oa, ob, ga, gb, w_bf, h2):
    t, d = h2.shape
    tm = min(t, 512)
    ra = pl.BlockSpec((tm, WIDTH_A), lambda i: (i, 0))
    rd = pl.BlockSpec((tm, d), lambda i: (i, 0))
    return pl.pallas_call(
        _mix_out_kernel,
        grid=(t // tm,),
        in_specs=[ra, ra,
                  pl.BlockSpec((1, WIDTH_A), lambda i: (0, 0)),
                  pl.BlockSpec((1, WIDTH_B), lambda i: (0, 0)),
                  pl.BlockSpec(w_bf.shape, lambda i: (0, 0)),
                  rd],
        out_specs=rd,
        out_shape=jax.ShapeDtypeStruct((t, d), F32),
        compiler_params=_cparams("parallel"),
        name="mix_out_proj",
    )(oa, ob, ga.reshape(1, -1), gb.reshape(1, -1), w_bf, h2)


def _peer_scores_kernel(h_ref, g_ref, wq_ref, sub_ref, xt_ref, scnh_ref, s2_ref):
    nh, nk = PEER_HEADS, PEER_N_KEYS
    xn = _rms(h_ref[...], g_ref[...])
    xt_ref[...] = xn.T.astype(BF16)
    qr = _dot(xn.astype(BF16), wq_ref[...]).astype(BF16)
    for h in range(nh):
        for c in range(2):
            col = (h * 2 + c) * nk
            sc = _dot_nt(sub_ref[h, c], qr[:, col:col + nk])
            for lt in range(s2_ref.shape[0]):
                sl = sc[:, lt * LANES:(lt + 1) * LANES]
                scnh_ref[lt, pl.ds(c * nk * nh + h, nk, stride=nh), :] = sl
                if c == 1:
                    s2_ref[lt, h * nk:(h + 1) * nk, :] = sl


def _peer_scores(h2, g, wq_bf, sub_bf):
    t, d = h2.shape
    nh, _, nk, _ = sub_bf.shape
    tm = min(t, 512)
    return pl.pallas_call(
        _peer_scores_kernel,
        grid=(t // tm,),
        in_specs=[pl.BlockSpec((tm, d), lambda i: (i, 0)),
                  pl.BlockSpec((1, d), lambda i: (0, 0)),
                  pl.BlockSpec(wq_bf.shape, lambda i: (0, 0)),
                  pl.BlockSpec(sub_bf.shape, lambda i: (0, 0, 0, 0))],
        out_specs=[pl.BlockSpec((d, tm), lambda i: (0, i)),
                   pl.BlockSpec((tm // LANES, 2 * nk * nh, LANES), lambda i: (i, 0, 0)),
                   pl.BlockSpec((tm // LANES, nh * nk, LANES), lambda i: (i, 0, 0))],
        out_shape=[jax.ShapeDtypeStruct((d, t), BF16),
                   jax.ShapeDtypeStruct((t // LANES, 2 * nk * nh, LANES), F32),
                   jax.ShapeDtypeStruct((t // LANES, nh * nk, LANES), F32)],
        compiler_params=_cparams("parallel"),
        name="peer_scores",
    )(h2, g.reshape(1, d), wq_bf, sub_bf)


def _oddeven_merge_sort_pairs(n):
    pairs = []

    def merge(lo, m, r):
        step = r * 2
        if step < m:
            merge(lo, m, step)
            merge(lo + r, m, step)
            for i in range(lo + r, lo + m - r, step):
                pairs.append((i, i + r))
        else:
            pairs.append((lo, lo + r))

    def sort(lo, m):
        if m > 1:
            h = m // 2
            sort(lo, h)
            sort(lo + h, h)
            merge(lo, m, 1)

    sort(0, n)
    return pairs


_SORT16 = _oddeven_merge_sort_pairs(PEER_TOPK)


def _sort_desc(vals):
    vals = list(vals)
    for i, j in _SORT16:
        a, b = vals[i], vals[j]
        vals[i], vals[j] = jnp.maximum(a, b), jnp.minimum(a, b)
    return vals


def _merge_top(a, b):
    n = len(a)
    c = [jnp.maximum(a[i], b[n - 1 - i]) for i in range(n)]
    stride = n // 2
    while stride >= 1:
        for i in range(n):
            if (i // stride) % 2 == 0:
                x, y = c[i], c[i + stride]
                c[i], c[i + stride] = jnp.maximum(x, y), jnp.minimum(x, y)
        stride //= 2
    return c


def _top_sorted(vals):
    k = PEER_TOPK
    groups = [_sort_desc(vals[i:i + k]) for i in range(0, len(vals), k)]
    while len(groups) > 1:
        groups = [_merge_top(groups[i], groups[i + 1]) for i in range(0, len(groups), 2)]
    return groups[0]


def _peer_select_kernel(scnh_ref, s2_ref, c1_ref, g1_ref, e2_ref, *, nchunk):
    nk, nh, k = PEER_N_KEYS, PEER_HEADS, PEER_TOPK
    big = -NEG_INF
    for lc in range(nchunk):
        x1 = [scnh_ref[lc, n * nh:(n + 1) * nh, :] for n in range(nk)]
        x2 = [scnh_ref[lc, (nk + n) * nh:(nk + n + 1) * nh, :] for n in range(nk)]
        a = _top_sorted(x1)
        b = _top_sorted(x2)
        cand = [a[i] + b[j] for i in range(k) for j in range(k) if (i + 1) * (j + 1) <= k]
        pad = [jnp.full_like(a[0], NEG_INF)] * (-len(cand) % k)
        top = _top_sorted(cand + pad)
        c16 = top[k - 1]
        c17 = functools.reduce(jnp.maximum, [jnp.where(c < c16, c, NEG_INF) for c in cand])
        thr = 0.5 * (c16 + c17)
        mx = a[0] + b[0]
        z = functools.reduce(lambda u, v: u + v, [jnp.where(c > thr, jnp.exp(c - mx), 0.0) for c in cand])
        inv_z = 1.0 / z
        for n in range(nk):
            keep = x1[n] >= a[k - 1]
            c1_ref[lc, n * nh:(n + 1) * nh, :] = jnp.where(keep, jnp.exp(thr - x1[n] - b[0]) * inv_z, big)
            g1_ref[lc, n * nh:(n + 1) * nh, :] = jnp.exp(x1[n] - a[0])
        for h in range(nh):
            s2 = s2_ref[lc, h * nk:(h + 1) * nk, :]
            e = jnp.exp(s2 - b[0][h:h + 1, :]) * inv_z[h:h + 1, :]
            e2_ref[lc, h * nk:(h + 1) * nk, :] = jnp.where(s2 >= b[k - 1][h:h + 1, :], e, 0.0)


def _peer_select(scnh, s2):
    nt, rows, _ = s2.shape
    nchunk = min(nt, 2)
    spec = pl.BlockSpec((nchunk, rows, LANES), lambda i: (i, 0, 0))
    return pl.pallas_call(
        functools.partial(_peer_select_kernel, nchunk=nchunk),
        grid=(nt // nchunk,),
        in_specs=[pl.BlockSpec((nchunk, scnh.shape[1], LANES), lambda i: (i, 0, 0)), spec],
        out_specs=[spec, spec, spec],
        out_shape=[jax.ShapeDtypeStruct(s2.shape, F32)] * 3,
        compiler_params=_cparams("parallel"),
        name="peer_select",
    )(scnh, s2)


def _gelu(x):
    return 0.5 * x * (1.0 + lax.erf(x * (2.0 ** -0.5)))


def _peer_mix_kernel(xt_ref, u_ref, vt_ref, e2_ref, c1_ref, g1_ref, out_ref, hid_ref, a_ref, *, te, tm, cw):
    nk, nh = PEER_N_KEYS, PEER_HEADS
    nslab = te // nk
    jrows = nk // 2
    tiles = cw // LANES

    @pl.when(pl.program_id(1) == 0)
    def _():
        out_ref[...] = jnp.zeros_like(out_ref)
        hid_ref[...] = jnp.zeros_like(hid_ref)

    for c in range(tm // cw):
        for l in range(tiles):
            lt = c * tiles + l
            for jh in range(2):
                w = [jnp.zeros((jrows, LANES), F32) for _ in range(nslab)]
                for h in range(nh):
                    r0 = h * nk + jh * jrows
                    e2 = e2_ref[lt, r0:r0 + jrows, :]
                    for ii in range(nslab):
                        r = ii * nh + h
                        sel = e2 >= c1_ref[lt, r:r + 1, :]
                        w[ii] = w[ii] + jnp.where(sel, e2 * g1_ref[lt, r:r + 1, :], 0.0)
                for ii in range(nslab):
                    r0 = ii * nk + jh * jrows
                    a = _gelu(hid_ref[lt, r0:r0 + jrows, :]) * w[ii]
                    a_ref[r0:r0 + jrows, lt * LANES:(lt + 1) * LANES] = a.astype(BF16)
        out_ref[:, c * cw:(c + 1) * cw] += _dot(vt_ref[...], a_ref[:, c * cw:(c + 1) * cw])
        hid = _dot(u_ref[...], xt_ref[:, c * cw:(c + 1) * cw])
        for l in range(tiles):
            hid_ref[c * tiles + l] = hid[:, l * LANES:(l + 1) * LANES]


def _peer_mix(xt, u_bf, vt_bf, e2, c1, g1):
    d, t = xt.shape
    n_exp = u_bf.shape[0]
    nt, rows, _ = e2.shape
    tm = min(t, 1024)
    te = 512
    cw = min(tm, 256)
    n = n_exp // te
    crow = te // PEER_N_KEYS * PEER_HEADS
    once = pl.Buffered(1)
    tok = pl.BlockSpec((tm // LANES, rows, LANES), lambda i, s: (i, 0, 0), pipeline_mode=once)
    sel = pl.BlockSpec((tm // LANES, crow, LANES), lambda i, s: (i, jnp.maximum(s - 1, 0), 0))
    return pl.pallas_call(
        functools.partial(_peer_mix_kernel, te=te, tm=tm, cw=cw),
        grid=(t // tm, n + 1),
        in_specs=[pl.BlockSpec((d, tm), lambda i, s: (0, i), pipeline_mode=once),
                  pl.BlockSpec((te, d), lambda i, s: (jnp.minimum(s, n - 1), 0)),
                  pl.BlockSpec((d, te), lambda i, s: (0, jnp.maximum(s - 1, 0))),
                  tok, sel, sel],
        out_specs=pl.BlockSpec((d, tm), lambda i, s: (0, i)),
        out_shape=jax.ShapeDtypeStruct((d, t), F32),
        scratch_shapes=[pltpu.VMEM((tm // LANES, te, LANES), F32), pltpu.VMEM((te, tm), BF16)],
        compiler_params=_cparams("parallel", "arbitrary"),
        name="peer_mix",
    )(xt, u_bf, vt_bf, e2, c1, g1)


def _ple_kernel(h_ref, pt_ref, g_ref, wg_ref, p_ref, wp_ref, fn_ref, out_ref, *, final):
    h = h_ref[...] + pt_ref[...].T
    hn = _rms(h, g_ref[...]).astype(BF16)
    gate = jax.nn.sigmoid(_dot(hn, wg_ref[...]))
    emb = _dot(p_ref[...].astype(BF16), wp_ref[...])
    y = h + emb * gate
    if final:
        y = _rms(y, fn_ref[...])
    out_ref[...] = y


def _ple(h2, peer_t, g, wg_bf, p2, wp_bf, fn, final):
    t, d = h2.shape
    tm = min(t, 256)
    rd = pl.BlockSpec((tm, d), lambda i: (i, 0))
    one = pl.BlockSpec((1, d), lambda i: (0, 0))
    full = lambda a: pl.BlockSpec(a.shape, lambda i: (0, 0))
    return pl.pallas_call(
        functools.partial(_ple_kernel, final=final),
        grid=(t // tm,),
        in_specs=[rd, pl.BlockSpec((d, tm), lambda i: (0, i)), one, full(wg_bf),
                  pl.BlockSpec((tm, p2.shape[1]), lambda i: (i, 0)), full(wp_bf), one],
        out_specs=rd,
        out_shape=jax.ShapeDtypeStruct((t, d), F32),
        compiler_params=_cparams("parallel"),
        name="ple_final" if final else "ple",
    )(h2, peer_t, g.reshape(1, d), wg_bf, p2, wp_bf, fn.reshape(1, d))


def kernel(x, p, positions, attn_norm, w_in, out_norm_a, out_norm_b, w_out, ffn_norm, peer_wq, peer_subkeys,
           peer_u, peer_v, ple_norm, ple_gate, ple_proj, final_norm):
    b, s, d = x.shape
    depth = w_in.shape[0]
    t = b * s
    assert s % (LANES * max(dil for _, dil in DILATED_PATTERNS)) == 0
    assert all(win // dil == HEAD_DIM for win, dil in DILATED_PATTERNS)
    cos, sin = _rope_tables(positions)
    h = x.reshape(t, d)
    for i in range(depth):
        pa, pb = _in_proj(h, attn_norm[i], w_in[i].astype(BF16), cos, sin)
        oa = _dilated_attn(pa, b, s)
        ob = _sb_attn(pb, b, s)
        h = _mix_out(oa, ob, out_norm_a[i], out_norm_b[i], w_out[i].astype(BF16), h)
        xt, scnh, s2 = _peer_scores(h, ffn_norm[i], peer_wq[i].astype(BF16), peer_subkeys[i].astype(BF16))
        c1, g1, e2 = _peer_select(scnh, s2)
        peer_t = _peer_mix(xt, peer_u[i].astype(BF16), peer_v[i].T.astype(BF16), e2, c1, g1)
        h = _ple(h, peer_t, ple_norm[i], ple_gate[i].astype(BF16), p[i].reshape(t, -1),
                 ple_proj[i].astype(BF16), final_norm, final=(i == depth - 1))
    return h.reshape(b, s, d)
```

```python
import functools
import math

import jax
import jax.numpy as jnp
from jax import lax
from jax.experimental import pallas as pl
from jax.experimental.pallas import tpu as pltpu

F32 = jnp.float32
BF16 = jnp.bfloat16

HEAD_DIM = 128
N_HEADS_A = 8
N_HEADS_B = 8
WIDTH_A = N_HEADS_A * HEAD_DIM
WIDTH_B = N_HEADS_B * HEAD_DIM
DILATED_PATTERNS = ((128, 1), (512, 4), (2048, 16))
ROPE_THETA = 10000.0
PEER_HEADS = 8
PEER_N_KEYS = 128
PEER_TOPK = 16
NORM_EPS = 1e-6
NEG_INF = -1e30

LANES = 128
SUBLANES = 8
VMEM_LIMIT = 56 * 1024 * 1024
F32_EXP_ZERO = -104.0
LOG2E = math.log2(math.e)


def _cparams(*sem):
    return pltpu.CompilerParams(dimension_semantics=sem, vmem_limit_bytes=VMEM_LIMIT)


def _dot(a, b):
    return jnp.dot(a, b, preferred_element_type=F32)


def _dot_nt(a, b):
    return lax.dot_general(a, b, (((1,), (1,)), ((), ())), preferred_element_type=F32)


def _rms(x, g):
    return x * lax.rsqrt(jnp.mean(x * x, axis=-1, keepdims=True) + NORM_EPS) * g


def _rope_kernel(pos_ref, freq_ref, sign_ref, cos_ref, sin_ref):
    ang = pos_ref[...].astype(F32) * freq_ref[...]
    cos_ref[...] = jnp.cos(ang)
    sin_ref[...] = jnp.sin(ang) * sign_ref[...]


def _rope_tables(positions):
    t = positions.size
    half = HEAD_DIM // 2
    inv_freq = ROPE_THETA ** (-jnp.arange(half, dtype=F32) / half)
    freq = jnp.concatenate([inv_freq, inv_freq]).reshape(1, HEAD_DIM)
    sign = jnp.concatenate([-jnp.ones((half,), F32), jnp.ones((half,), F32)]).reshape(1, HEAD_DIM)
    tm = min(t, 1024)
    row = pl.BlockSpec((tm, HEAD_DIM), lambda i: (i, 0))
    one = pl.BlockSpec((1, HEAD_DIM), lambda i: (0, 0))
    return pl.pallas_call(
        _rope_kernel,
        grid=(t // tm,),
        in_specs=[pl.BlockSpec((tm, 1), lambda i: (i, 0)), one, one],
        out_specs=[row, row],
        out_shape=[jax.ShapeDtypeStruct((t, HEAD_DIM), F32)] * 2,
        compiler_params=_cparams("parallel"),
        name="rope_tables",
    )(positions.reshape(t, 1), freq, sign)


def _in_proj_kernel(x_ref, g_ref, w_ref, cos_ref, sin_ref, pa_ref, pb_ref, xn_ref, *, scale):
    j = pl.program_id(1)

    @pl.when(j == 0)
    def _():
        xn_ref[...] = _rms(x_ref[...], g_ref[...]).astype(BF16)

    acc = _dot(xn_ref[...], w_ref[...])

    @pl.when(j < 2)
    def _():
        sc = jnp.where(j == 0, scale, 1.0).astype(F32)
        c = cos_ref[...] * sc
        s = sin_ref[...] * sc
        for hh in range(N_HEADS_A):
            cs = slice(hh * HEAD_DIM, (hh + 1) * HEAD_DIM)
            a = acc[:, cs]
            pa_ref[:, cs] = a * c + pltpu.roll(a, HEAD_DIM // 2, 1) * s

    @pl.when(j == 2)
    def _():
        pa_ref[...] = acc

    @pl.when(j >= 3)
    def _():
        sc = jnp.where(j == 3, scale * LOG2E, 1.0).astype(F32)
        pb_ref[...] = (acc * sc).astype(BF16)


def _in_proj(h2, g, w_bf, cos, sin):
    t, d = h2.shape
    n = w_bf.shape[1]
    tm = min(t, 1024)
    tn = WIDTH_A
    na = 3
    return pl.pallas_call(
        functools.partial(_in_proj_kernel, scale=HEAD_DIM ** -0.5),
        grid=(t // tm, n // tn),
        in_specs=[
            pl.BlockSpec((tm, d), lambda i, j: (i, 0)),
            pl.BlockSpec((1, d), lambda i, j: (0, 0)),
            pl.BlockSpec((d, tn), lambda i, j: (0, j)),
            pl.BlockSpec((tm, HEAD_DIM), lambda i, j: (i, 0)),
            pl.BlockSpec((tm, HEAD_DIM), lambda i, j: (i, 0)),
        ],
        out_specs=[pl.BlockSpec((tm, tn), lambda i, j: (i, jnp.minimum(j, na - 1))),
                   pl.BlockSpec((tm, tn), lambda i, j: (i, jnp.maximum(j - na, 0)))],
        out_shape=[jax.ShapeDtypeStruct((t, na * tn), F32), jax.ShapeDtypeStruct((t, n - na * tn), BF16)],
        scratch_shapes=[pltpu.VMEM((tm, d), BF16)],
        compiler_params=_cparams("parallel", "arbitrary"),
        name="in_proj",
    )(h2, g.reshape(1, d), w_bf, cos, sin)


def _dilated_kernel(q_ref, k_ref, v_ref, o_ref, num, den, mac, bias_ref, *, seq):
    w = HEAD_DIM
    diff = (lax.broadcasted_iota(jnp.int32, (w, 2 * w), 0) - lax.broadcasted_iota(jnp.int32, (w, 2 * w), 1))
    for first_block, off in ((0, 0), (1, w)):
        bias_ref[first_block] = jnp.where(jnp.abs(diff + (off - w // 2)) <= w // 2, 0.0, NEG_INF)
    ones = jnp.ones((2 * w, w), BF16)

    def block(blk, r, d, first):
        span = w * d
        pblk = jnp.maximum(blk - 1, 0)
        qwin = pl.ds(pl.multiple_of(blk * span, span), span)
        kwin = pl.ds(pl.multiple_of(pblk * span, span), 2 * span)
        rows = (lambda n: pl.ds(r, n, stride=d)) if d > 1 else (lambda n: pl.ds(0, n))
        q = q_ref.at[qwin][rows(w), :].astype(BF16)
        k2 = k_ref.at[kwin][rows(2 * w), :].astype(BF16)
        v2 = v_ref.at[kwin][rows(2 * w), :].astype(BF16)
        s = _dot_nt(q, k2) + bias_ref[blk - pblk]
        m = jnp.max(s, axis=-1, keepdims=True)
        e = jnp.exp(s - m).astype(BF16)
        pv = _dot(e, jnp.concatenate([v2, ones], axis=1))
        if first:
            num.at[qwin][rows(w), :] = pv[:, :w]
            den.at[qwin][rows(w), :] = pv[:, w:]
            mac.at[qwin][rows(w), :] = jnp.broadcast_to(m, (w, w))
        else:
            m0 = mac.at[qwin][rows(w), :]
            mn = jnp.maximum(m0, m)
            a0 = jnp.exp(m0 - mn)
            a1 = jnp.exp(m - mn)
            num.at[qwin][rows(w), :] = a0 * num.at[qwin][rows(w), :] + a1 * pv[:, :w]
            den.at[qwin][rows(w), :] = a0 * den.at[qwin][rows(w), :] + a1 * pv[:, w:]
            mac.at[qwin][rows(w), :] = mn

    for pi, (_, d) in enumerate(DILATED_PATTERNS):
        nblk = seq // d // w
        ub = max(1, 8 // d)

        def body(g, carry, d=d, ub=ub, first=(pi == 0)):
            for u in range(ub):
                for r in range(d):
                    block(g * ub + u, r, d, first)
            return carry

        lax.fori_loop(0, nblk // ub, body, 0)
    o_ref[...] = (num[...] / den[...]).astype(BF16)


def _dilated_attn(pa, b, s):
    view = pa.reshape(b, s, pa.shape[1])

    def in_map(x):
        return lambda bi, h: (bi, 0, x * N_HEADS_A + h)

    spec = lambda x: pl.BlockSpec((None, s, HEAD_DIM), in_map(x))
    o = pl.pallas_call(
        functools.partial(_dilated_kernel, seq=s),
        grid=(b, N_HEADS_A),
        in_specs=[spec(0), spec(1), spec(2)],
        out_specs=pl.BlockSpec((None, s, HEAD_DIM), lambda bi, h: (bi, 0, h)),
        out_shape=jax.ShapeDtypeStruct((b, s, WIDTH_A), BF16),
        scratch_shapes=[pltpu.VMEM((s, HEAD_DIM), F32), pltpu.VMEM((s, HEAD_DIM), F32),
                        pltpu.VMEM((s, HEAD_DIM), F32), pltpu.VMEM((2, HEAD_DIM, 2 * HEAD_DIM), F32)],
        compiler_params=_cparams("parallel", "parallel"),
        name="dilated_attn",
    )(view, view, view)
    return o.reshape(b * s, WIDTH_A)


def _sb_kernel(q_ref, k_ref, v_ref, o_ref, *, blk, nq, hg):
    row = lax.broadcasted_iota(jnp.int32, (blk, blk), 0)
    col = lax.broadcasted_iota(jnp.int32, (blk, blk), 1)
    tri = jnp.where(row > col, 1.0, 0.0).astype(BF16)
    causal = col < row
    heads = [slice(h * HEAD_DIM, (h + 1) * HEAD_DIM) for h in range(hg)]

    def qblock(i, carry):
        q0 = pl.multiple_of(i * blk, blk)
        qs = [q_ref[pl.ds(q0, blk), cs] for cs in heads]

        def step(st, diagonal):
            kb, cs_, accs = st
            k0 = pl.multiple_of(kb * blk, blk)
            logits, parts, new_c, new_acc = [], [], [], []
            for q, c, cs in zip(qs, cs_, heads):
                z = _dot_nt(q, k_ref[pl.ds(k0, blk), cs])
                sp = jnp.maximum(z, 0.0) + jnp.log2(1.0 + jnp.exp2(-jnp.abs(z)))
                lm = jnp.where(causal, -sp, 0.0) if diagonal else -sp
                hi = lm.astype(BF16)
                parts += [hi, (lm - hi.astype(F32)).astype(BF16)]
                logits.append(z - sp + c)
                new_c.append(c + jnp.sum(lm, axis=-1, keepdims=True))
            later = _dot(jnp.concatenate(parts, axis=0), tri)
            for h, (lg, acc, cs) in enumerate(zip(logits, accs, heads)):
                a = jnp.exp2(lg + later[2 * h * blk:(2 * h + 1) * blk] + later[(2 * h + 1) * blk:(2 * h + 2) * blk])
                if diagonal:
                    a = jnp.where(causal, a, 0.0)
                new_acc.append(acc + _dot(a.astype(BF16), v_ref[pl.ds(k0, blk), cs]))
            return kb - 1, tuple(new_c), tuple(new_acc)

        def cond(st):
            kb, cs_, _ = st
            live = jnp.max(functools.reduce(jnp.maximum, cs_)) > F32_EXP_ZERO * LOG2E
            return jnp.logical_and(kb >= 0, live)

        init = (i, tuple(jnp.zeros((blk, 1), F32) for _ in heads),
                tuple(jnp.zeros((blk, HEAD_DIM), F32) for _ in heads))
        _, _, accs = lax.while_loop(cond, lambda st: step(st, False), step(init, True))
        for acc, cs in zip(accs, heads):
            o_ref[pl.ds(q0, blk), cs] = acc.astype(BF16)
        return carry

    lax.fori_loop(0, nq, qblock, 0)


def _sb_attn(pb, b, s):
    view = pb.reshape(b, s, pb.shape[1])
    blk = min(s, 256)
    hg = 4
    cw = hg * HEAD_DIM
    ngrp = N_HEADS_B // hg

    def in_map(x):
        return lambda bi, g: (bi, 0, x * ngrp + g)

    o = pl.pallas_call(
        functools.partial(_sb_kernel, blk=blk, nq=s // blk, hg=hg),
        grid=(b, ngrp),
        in_specs=[pl.BlockSpec((None, s, cw), in_map(0)),
                  pl.BlockSpec((None, s, cw), in_map(1)),
                  pl.BlockSpec((None, s, cw), in_map(2))],
        out_specs=pl.BlockSpec((None, s, cw), lambda bi, g: (bi, 0, g)),
        out_shape=jax.ShapeDtypeStruct((b, s, WIDTH_B), BF16),
        compiler_params=_cparams("parallel", "parallel"),
        name="stick_breaking",
    )(view, view, view)
    return o.reshape(b * s, WIDTH_B)


def _mix_out_kernel(oa_ref, ob_ref, ga_ref, gb_ref, w_ref, h_ref, out_ref):
    na = _rms(oa_ref[...].astype(F32), ga_ref[...]).astype(BF16)
    nb = _rms(ob_ref[...].astype(F32), gb_ref[...]).astype(BF16)
    y = _dot(na, w_ref[:WIDTH_A, :]) + _dot(nb, w_ref[WIDTH_A:, :])
    out_ref[...] = h_ref[...] + y


def _mix_out(oa, ob, ga, gb, w_bf, h2):
    t, d = h2.shape
    tm = min(t, 512)
    ra = pl.BlockSpec((tm, WIDTH_A), lambda i: (i, 0))
    rd = pl.BlockSpec((tm, d), lambda i: (i, 0))
    return pl.pallas_call(
        _mix_out_kernel,
        grid=(t // tm,),
        in_specs=[ra, ra,
                  pl.BlockSpec((1, WIDTH_A), lambda i: (0, 0)),
                  pl.BlockSpec((1, WIDTH_B), lambda i: (0, 0)),
                  pl.BlockSpec(w_bf.shape, lambda i: (0, 0)),
                  rd],
        out_specs=rd,
        out_shape=jax.ShapeDtypeStruct((t, d), F32),
        compiler_params=_cparams("parallel"),
        name="mix_out_proj",
    )(oa, ob, ga.reshape(1, -1), gb.reshape(1, -1), w_bf, h2)


def _peer_scores_kernel(h_ref, g_ref, wq_ref, sub_ref, xt_ref, scnh_ref, s2_ref):
    nh, nk = PEER_HEADS, PEER_N_KEYS
    xn = _rms(h_ref[...], g_ref[...])
    xt_ref[...] = xn.T.astype(BF16)
    qr = _dot(xn.astype(BF16), wq_ref[...]).astype(BF16)
    for h in range(nh):
        for c in range(2):
            col = (h * 2 + c) * nk
            sc = _dot_nt(sub_ref[h, c], qr[:, col:col + nk])
            for lt in range(s2_ref.shape[0]):
                sl = sc[:, lt * LANES:(lt + 1) * LANES]
                scnh_ref[lt, pl.ds(c * nk * nh + h, nk, stride=nh), :] = sl
                if c == 1:
                    s2_ref[lt, h * nk:(h + 1) * nk, :] = sl


def _peer_scores(h2, g, wq_bf, sub_bf):
    t, d = h2.shape
    nh, _, nk, _ = sub_bf.shape
    tm = min(t, 512)
    return pl.pallas_call(
        _peer_scores_kernel,
        grid=(t // tm,),
        in_specs=[pl.BlockSpec((tm, d), lambda i: (i, 0)),
                  pl.BlockSpec((1, d), lambda i: (0, 0)),
                  pl.BlockSpec(wq_bf.shape, lambda i: (0, 0)),
                  pl.BlockSpec(sub_bf.shape, lambda i: (0, 0, 0, 0))],
        out_specs=[pl.BlockSpec((d, tm), lambda i: (0, i)),
                   pl.BlockSpec((tm // LANES, 2 * nk * nh, LANES), lambda i: (i, 0, 0)),
                   pl.BlockSpec((tm // LANES, nh * nk, LANES), lambda i: (i, 0, 0))],
        out_shape=[jax.ShapeDtypeStruct((d, t), BF16),
                   jax.ShapeDtypeStruct((t // LANES, 2 * nk * nh, LANES), F32),
                   jax.ShapeDtypeStruct((t // LANES, nh * nk, LANES), F32)],
        compiler_params=_cparams("parallel"),
        name="peer_scores",
    )(h2, g.reshape(1, d), wq_bf, sub_bf)


def _oddeven_merge_sort_pairs(n):
    pairs = []

    def merge(lo, m, r):
        step = r * 2
        if step < m:
            merge(lo, m, step)
            merge(lo + r, m, step)
            for i in range(lo + r, lo + m - r, step):
                pairs.append((i, i + r))
        else:
            pairs.append((lo, lo + r))

    def sort(lo, m):
        if m > 1:
            h = m // 2
            sort(lo, h)
            sort(lo + h, h)
            merge(lo, m, 1)

    sort(0, n)
    return pairs


_SORT16 = _oddeven_merge_sort_pairs(PEER_TOPK)


def _sort_desc(vals):
    vals = list(vals)
    for i, j in _SORT16:
        a, b = vals[i], vals[j]
        vals[i], vals[j] = jnp.maximum(a, b), jnp.minimum(a, b)
    return vals


def _merge_top(a, b):
    n = len(a)
    c = [jnp.maximum(a[i], b[n - 1 - i]) for i in range(n)]
    stride = n // 2
    while stride >= 1:
        for i in range(n):
            if (i // stride) % 2 == 0:
                x, y = c[i], c[i + stride]
                c[i], c[i + stride] = jnp.maximum(x, y), jnp.minimum(x, y)
        stride //= 2
    return c


def _top_sorted(vals):
    k = PEER_TOPK
    groups = [_sort_desc(vals[i:i + k]) for i in range(0, len(vals), k)]
    while len(groups) > 1:
        groups = [_merge_top(groups[i], groups[i + 1]) for i in range(0, len(groups), 2)]
    return groups[0]


def _peer_select_kernel(scnh_ref, s2_ref, c1_ref, g1_ref, e2_ref, *, nchunk):
    nk, nh, k = PEER_N_KEYS, PEER_HEADS, PEER_TOPK
    big = -NEG_INF
    for lc in range(nchunk):
        x1 = [scnh_ref[lc, n * nh:(n + 1) * nh, :] for n in range(nk)]
        x2 = [scnh_ref[lc, (nk + n) * nh:(nk + n + 1) * nh, :] for n in range(nk)]
        a = _top_sorted(x1)
        b = _top_sorted(x2)
        cand = [a[i] + b[j] for i in range(k) for j in range(k) if (i + 1) * (j + 1) <= k]
        pad = [jnp.full_like(a[0], NEG_INF)] * (-len(cand) % k)
        top = _top_sorted(cand + pad)
        c16 = top[k - 1]
        c17 = functools.reduce(jnp.maximum, [jnp.where(c < c16, c, NEG_INF) for c in cand])
        thr = 0.5 * (c16 + c17)
        mx = a[0] + b[0]
        z = functools.reduce(lambda u, v: u + v, [jnp.where(c > thr, jnp.exp(c - mx), 0.0) for c in cand])
        inv_z = 1.0 / z
        for n in range(nk):
            keep = x1[n] >= a[k - 1]
            c1_ref[lc, n * nh:(n + 1) * nh, :] = jnp.where(keep, jnp.exp(thr - x1[n] - b[0]) * inv_z, big)
            g1_ref[lc, n * nh:(n + 1) * nh, :] = jnp.exp(x1[n] - a[0])
        for h in range(nh):
            s2 = s2_ref[lc, h * nk:(h + 1) * nk, :]
            e = jnp.exp(s2 - b[0][h:h + 1, :]) * inv_z[h:h + 1, :]
            e2_ref[lc, h * nk:(h + 1) * nk, :] = jnp.where(s2 >= b[k - 1][h:h + 1, :], e, 0.0)


def _peer_select(scnh, s2):
    nt, rows, _ = s2.shape
    nchunk = min(nt, 2)
    spec = pl.BlockSpec((nchunk, rows, LANES), lambda i: (i, 0, 0))
    return pl.pallas_call(
        functools.partial(_peer_select_kernel, nchunk=nchunk),
        grid=(nt // nchunk,),
        in_specs=[pl.BlockSpec((nchunk, scnh.shape[1], LANES), lambda i: (i, 0, 0)), spec],
        out_specs=[spec, spec, spec],
        out_shape=[jax.ShapeDtypeStruct(s2.shape, F32)] * 3,
        compiler_params=_cparams("parallel"),
        name="peer_select",
    )(scnh, s2)


def _gelu(x):
    return 0.5 * x * (1.0 + lax.erf(x * (2.0 ** -0.5)))


def _peer_mix_kernel(xt_ref, u_ref, vt_ref, e2_ref, c1_ref, g1_ref, out_ref, hid_ref, a_ref, *, te, tm, cw):
    nk, nh = PEER_N_KEYS, PEER_HEADS
    nslab = te // nk
    jrows = nk // 2
    tiles = cw // LANES

    @pl.when(pl.program_id(1) == 0)
    def _():
        out_ref[...] = jnp.zeros_like(out_ref)
        hid_ref[...] = jnp.zeros_like(hid_ref)

    for c in range(tm // cw):
        for l in range(tiles):
            lt = c * tiles + l
            for jh in range(2):
                w = [jnp.zeros((jrows, LANES), F32) for _ in range(nslab)]
                for h in range(nh):
                    r0 = h * nk + jh * jrows
                    e2 = e2_ref[lt, r0:r0 + jrows, :]
                    for ii in range(nslab):
                        r = ii * nh + h
                        sel = e2 >= c1_ref[lt, r:r + 1, :]
                        w[ii] = w[ii] + jnp.where(sel, e2 * g1_ref[lt, r:r + 1, :], 0.0)
                for ii in range(nslab):
                    r0 = ii * nk + jh * jrows
                    a = _gelu(hid_ref[lt, r0:r0 + jrows, :]) * w[ii]
                    a_ref[r0:r0 + jrows, lt * LANES:(lt + 1) * LANES] = a.astype(BF16)
        out_ref[:, c * cw:(c + 1) * cw] += _dot(vt_ref[...], a_ref[:, c * cw:(c + 1) * cw])
        hid = _dot(u_ref[...], xt_ref[:, c * cw:(c + 1) * cw])
        for l in range(tiles):
            hid_ref[c * tiles + l] = hid[:, l * LANES:(l + 1) * LANES]


def _peer_mix(xt, u_bf, vt_bf, e2, c1, g1):
    d, t = xt.shape
    n_exp = u_bf.shape[0]
    nt, rows, _ = e2.shape
    tm = min(t, 1024)
    te = 512
    cw = min(tm, 256)
    n = n_exp // te
    crow = te // PEER_N_KEYS * PEER_HEADS
    once = pl.Buffered(1)
    tok = pl.BlockSpec((tm // LANES, rows, LANES), lambda i, s: (i, 0, 0), pipeline_mode=once)
    sel = pl.BlockSpec((tm // LANES, crow, LANES), lambda i, s: (i, jnp.maximum(s - 1, 0), 0))
    return pl.pallas_call(
        functools.partial(_peer_mix_kernel, te=te, tm=tm, cw=cw),
        grid=(t // tm, n + 1),
        in_specs=[pl.BlockSpec((d, tm), lambda i, s: (0, i), pipeline_mode=once),
                  pl.BlockSpec((te, d), lambda i, s: (jnp.minimum(s, n - 1), 0)),
                  pl.BlockSpec((d, te), lambda i, s: (0, jnp.maximum(s - 1, 0))),
                  tok, sel, sel],
        out_specs=pl.BlockSpec((d, tm), lambda i, s: (0, i)),
        out_shape=jax.ShapeDtypeStruct((d, t), F32),
        scratch_shapes=[pltpu.VMEM((tm // LANES, te, LANES), F32), pltpu.VMEM((te, tm), BF16)],
        compiler_params=_cparams("parallel", "arbitrary"),
        name="peer_mix",
    )(xt, u_bf, vt_bf, e2, c1, g1)


def _ple_kernel(h_ref, pt_ref, g_ref, wg_ref, p_ref, wp_ref, fn_ref, out_ref, *, final):
    h = h_ref[...] + pt_ref[...].T
    hn = _rms(h, g_ref[...]).astype(BF16)
    gate = jax.nn.sigmoid(_dot(hn, wg_ref[...]))
    emb = _dot(p_ref[...].astype(BF16), wp_ref[...])
    y = h + emb * gate
    if final:
        y = _rms(y, fn_ref[...])
    out_ref[...] = y


def _ple(h2, peer_t, g, wg_bf, p2, wp_bf, fn, final):
    t, d = h2.shape
    tm = min(t, 256)
    rd = pl.BlockSpec((tm, d), lambda i: (i, 0))
    one = pl.BlockSpec((1, d), lambda i: (0, 0))
    full = lambda a: pl.BlockSpec(a.shape, lambda i: (0, 0))
    return pl.pallas_call(
        functools.partial(_ple_kernel, final=final),
        grid=(t // tm,),
        in_specs=[rd, pl.BlockSpec((d, tm), lambda i: (0, i)), one, full(wg_bf),
                  pl.BlockSpec((tm, p2.shape[1]), lambda i: (i, 0)), full(wp_bf), one],
        out_specs=rd,
        out_shape=jax.ShapeDtypeStruct((t, d), F32),
        compiler_params=_cparams("parallel"),
        name="ple_final" if final else "ple",
    )(h2, peer_t, g.reshape(1, d), wg_bf, p2, wp_bf, fn.reshape(1, d))


def kernel(x, p, positions, attn_norm, w_in, out_norm_a, out_norm_b, w_out, ffn_norm, peer_wq, peer_subkeys,
           peer_u, peer_v, ple_norm, ple_gate, ple_proj, final_norm):
    b, s, d = x.shape
    depth = w_in.shape[0]
    t = b * s
    assert s % (LANES * max(dil for _, dil in DILATED_PATTERNS)) == 0
    assert all(win // dil == HEAD_DIM for win, dil in DILATED_PATTERNS)
    cos, sin = _rope_tables(positions)
    h = x.reshape(t, d)
    for i in range(depth):
        pa, pb = _in_proj(h, attn_norm[i], w_in[i].astype(BF16), cos, sin)
        oa = _dilated_attn(pa, b, s)
        ob = _sb_attn(pb, b, s)
        h = _mix_out(oa, ob, out_norm_a[i], out_norm_b[i], w_out[i].astype(BF16), h)
        xt, scnh, s2 = _peer_scores(h, ffn_norm[i], peer_wq[i].astype(BF16), peer_subkeys[i].astype(BF16))
        c1, g1, e2 = _peer_select(scnh, s2)
        peer_t = _peer_mix(xt, peer_u[i].astype(BF16), peer_v[i].T.astype(BF16), e2, c1, g1)
        h = _ple(h, peer_t, ple_norm[i], ple_gate[i].astype(BF16), p[i].reshape(t, -1),
                 ple_proj[i].astype(BF16), final_norm, final=(i == depth - 1))
    return h.reshape(b, s, d)
```

```python
import functools
import math

import jax
import jax.numpy as jnp
from jax import lax
from jax.experimental import pallas as pl
from jax.experimental.pallas import tpu as pltpu

F32 = jnp.float32
BF16 = jnp.bfloat16

HEAD_DIM = 128
N_HEADS_A = 8
N_HEADS_B = 8
WIDTH_A = N_HEADS_A * HEAD_DIM
WIDTH_B = N_HEADS_B * HEAD_DIM
DILATED_PATTERNS = ((128, 1), (512, 4), (2048, 16))
ROPE_THETA = 10000.0
PEER_HEADS = 8
PEER_N_KEYS = 128
PEER_TOPK = 16
NORM_EPS = 1e-6
NEG_INF = -1e30

LANES = 128
SUBLANES = 8
VMEM_LIMIT = 56 * 1024 * 1024
F32_EXP_ZERO = -104.0
LOG2E = math.log2(math.e)


def _cparams(*sem):
    return pltpu.CompilerParams(dimension_semantics=sem, vmem_limit_bytes=VMEM_LIMIT)


def _dot(a, b):
    return jnp.dot(a, b, preferred_element_type=F32)


def _dot_nt(a, b):
    return lax.dot_general(a, b, (((1,), (1,)), ((), ())), preferred_element_type=F32)


def _rms(x, g):
    return x * lax.rsqrt(jnp.mean(x * x, axis=-1, keepdims=True) + NORM_EPS) * g


def _rope_kernel(pos_ref, freq_ref, sign_ref, cos_ref, sin_ref):
    ang = pos_ref[...].astype(F32) * freq_ref[...]
    cos_ref[...] = jnp.cos(ang)
    sin_ref[...] = jnp.sin(ang) * sign_ref[...]


def _rope_tables(positions):
    t = positions.size
    half = HEAD_DIM // 2
    inv_freq = ROPE_THETA ** (-jnp.arange(half, dtype=F32) / half)
    freq = jnp.concatenate([inv_freq, inv_freq]).reshape(1, HEAD_DIM)
    sign = jnp.concatenate([-jnp.ones((half,), F32), jnp.ones((half,), F32)]).reshape(1, HEAD_DIM)
    tm = min(t, 1024)
    row = pl.BlockSpec((tm, HEAD_DIM), lambda i: (i, 0))
    one = pl.BlockSpec((1, HEAD_DIM), lambda i: (0, 0))
    return pl.pallas_call(
        _rope_kernel,
        grid=(t // tm,),
        in_specs=[pl.BlockSpec((tm, 1), lambda i: (i, 0)), one, one],
        out_specs=[row, row],
        out_shape=[jax.ShapeDtypeStruct((t, HEAD_DIM), F32)] * 2,
        compiler_params=_cparams("parallel"),
        name="rope_tables",
    )(positions.reshape(t, 1), freq, sign)


def _in_proj_kernel(x_ref, g_ref, w_ref, cos_ref, sin_ref, pa_ref, pb_ref, xn_ref, *, scale):
    j = pl.program_id(1)

    @pl.when(j == 0)
    def _():
        xn_ref[...] = _rms(x_ref[...], g_ref[...]).astype(BF16)

    acc = _dot(xn_ref[...], w_ref[...])

    @pl.when(j < 2)
    def _():
        sc = jnp.where(j == 0, scale, 1.0).astype(F32)
        c = cos_ref[...] * sc
        s = sin_ref[...] * sc
        for hh in range(N_HEADS_A):
            cs = slice(hh * HEAD_DIM, (hh + 1) * HEAD_DIM)
            a = acc[:, cs]
            pa_ref[:, cs] = a * c + pltpu.roll(a, HEAD_DIM // 2, 1) * s

    @pl.when(j == 2)
    def _():
        pa_ref[...] = acc

    @pl.when(j >= 3)
    def _():
        sc = jnp.where(j == 3, scale * LOG2E, 1.0).astype(F32)
        pb_ref[...] = (acc * sc).astype(BF16)


def _in_proj(h2, g, w_bf, cos, sin):
    t, d = h2.shape
    n = w_bf.shape[1]
    tm = min(t, 1024)
    tn = WIDTH_A
    na = 3
    return pl.pallas_call(
        functools.partial(_in_proj_kernel, scale=HEAD_DIM ** -0.5),
        grid=(t // tm, n // tn),
        in_specs=[
            pl.BlockSpec((tm, d), lambda i, j: (i, 0)),
            pl.BlockSpec((1, d), lambda i, j: (0, 0)),
            pl.BlockSpec((d, tn), lambda i, j: (0, j)),
            pl.BlockSpec((tm, HEAD_DIM), lambda i, j: (i, 0)),
            pl.BlockSpec((tm, HEAD_DIM), lambda i, j: (i, 0)),
        ],
        out_specs=[pl.BlockSpec((tm, tn), lambda i, j: (i, jnp.minimum(j, na - 1))),
                   pl.BlockSpec((tm, tn), lambda i, j: (i, jnp.maximum(j - na, 0)))],
        out_shape=[jax.ShapeDtypeStruct((t, na * tn), F32), jax.ShapeDtypeStruct((t, n - na * tn), BF16)],
        scratch_shapes=[pltpu.VMEM((tm, d), BF16)],
        compiler_params=_cparams("parallel", "arbitrary"),
        name="in_proj",
    )(h2, g.reshape(1, d), w_bf, cos, sin)


DEINT = 4


def _dilated_kernel(q_ref, k_ref, v_ref, o_ref, x4, num, den, mac, onat, bias_ref, *, seq):
    w = HEAD_DIM
    nr = DEINT
    cq = w // nr
    qi = lax.broadcasted_iota(jnp.int32, (w, 2 * w), 0)
    kj = lax.broadcasted_iota(jnp.int32, (w, 2 * w), 1)
    sh = cq.bit_length() - 1
    qpos = (qi, nr * (qi & (cq - 1)) + (qi >> sh))
    kpos = (kj, nr * (kj & (2 * cq - 1)) + (kj >> (sh + 1)))
    for order in range(2):
        for later_block, off in ((0, 0), (1, w)):
            delta = qpos[order] - kpos[order] + (off - w // 2)
            bias_ref[2 * order + later_block] = jnp.where(jnp.abs(delta) <= w // 2, 0.0, NEG_INF)
    ones = jnp.ones((2 * w, w), BF16)

    def deinterleave(c, carry):
        src = pl.ds(pl.multiple_of(c * (nr * w), nr * w), nr * w)
        dst = pl.ds(pl.multiple_of(c * w, w), w)
        for t, ref in enumerate((q_ref, k_ref, v_ref)):
            for r in range(nr):
                x4[t, r, dst, :] = ref.at[src][pl.ds(r, w, stride=nr), :]
        return carry

    lax.fori_loop(0, seq // (nr * w), deinterleave, 0)

    def attend(q, k2, v2, bias, acc, first):
        s = _dot_nt(q.astype(BF16), k2.astype(BF16)) + bias
        m = jnp.max(s, axis=-1, keepdims=True)
        e = jnp.exp(s - m).astype(BF16)
        pv = _dot(e, jnp.concatenate([v2.astype(BF16), ones], axis=1))
        (ld_n, st_n), (ld_d, st_d), (ld_m, st_m) = acc
        if first:
            st_n(pv[:, :w])
            st_d(pv[:, w:])
            st_m(jnp.broadcast_to(m, (w, w)))
        else:
            m0 = ld_m()
            mn = jnp.maximum(m0, m)
            a0 = jnp.exp(m0 - mn)
            a1 = jnp.exp(m - mn)
            st_n(a0 * ld_n() + a1 * pv[:, :w])
            st_d(a0 * ld_d() + a1 * pv[:, w:])
            st_m(mn)

    def block_d1(b):
        pb = jnp.maximum(b - 1, 0)
        qrow = pl.ds(pl.multiple_of(b * cq, cq), cq)
        krow = pl.ds(pl.multiple_of(pb * cq, cq), 2 * cq)
        q = jnp.concatenate([x4[0, r, qrow, :] for r in range(nr)], axis=0)
        k2 = jnp.concatenate([x4[1, r, krow, :] for r in range(nr)], axis=0)
        v2 = jnp.concatenate([x4[2, r, krow, :] for r in range(nr)], axis=0)

        def chunked(ref):
            def store(val):
                for r in range(nr):
                    ref[r, qrow, :] = val[r * cq:(r + 1) * cq]
            return (lambda: jnp.concatenate([ref[r, qrow, :] for r in range(nr)], axis=0)), store

        attend(q, k2, v2, bias_ref[2 + b - pb], [chunked(num), chunked(den), chunked(mac)], True)

    def block_dn(blk, r, sub, nsub):
        span = w * nsub
        pblk = jnp.maximum(blk - 1, 0)
        qwin = pl.ds(pl.multiple_of(blk * span, span), span)
        kwin = pl.ds(pl.multiple_of(pblk * span, span), 2 * span)
        rows = (lambda n: pl.ds(sub, n, stride=nsub)) if nsub > 1 else (lambda n: pl.ds(0, n))

        def strided(ref):
            view = ref.at[r, qwin]
            def store(val):
                view[rows(w), :] = val
            return (lambda: view[rows(w), :]), store

        attend(x4.at[0, r, qwin][rows(w), :], x4.at[1, r, kwin][rows(2 * w), :], x4.at[2, r, kwin][rows(2 * w), :],
               bias_ref[blk - pblk], [strided(num), strided(den), strided(mac)], False)

    for _, d in DILATED_PATTERNS:
        nblk = seq // d // w
        if d == 1:
            ub = 8

            def body(g, carry, ub=ub):
                for u in range(ub):
                    block_d1(g * ub + u)
                return carry
        else:
            nsub = d // nr
            ub = max(1, 8 // d)

            def body(g, carry, ub=ub, nsub=nsub):
                for u in range(ub):
                    for r in range(nr):
                        for sub in range(nsub):
                            block_dn(g * ub + u, r, sub, nsub)
                return carry

        lax.fori_loop(0, nblk // ub, body, 0)

    def interleave(c, carry):
        src = pl.ds(pl.multiple_of(c * w, w), w)
        dst = pl.ds(pl.multiple_of(c * (nr * w), nr * w), nr * w)
        for r in range(nr):
            onat.at[dst][pl.ds(r, w, stride=nr), :] = num[r, src, :] / den[r, src, :]
        return carry

    lax.fori_loop(0, seq // (nr * w), interleave, 0)
    o_ref[...] = onat[...].astype(BF16)


def _dilated_attn(pa, b, s):
    view = pa.reshape(b, s, pa.shape[1])

    def in_map(x):
        return lambda bi, h: (bi, 0, x * N_HEADS_A + h)

    spec = lambda x: pl.BlockSpec((None, s, HEAD_DIM), in_map(x))
    o = pl.pallas_call(
        functools.partial(_dilated_kernel, seq=s),
        grid=(b, N_HEADS_A),
        in_specs=[spec(0), spec(1), spec(2)],
        out_specs=pl.BlockSpec((None, s, HEAD_DIM), lambda bi, h: (bi, 0, h)),
        out_shape=jax.ShapeDtypeStruct((b, s, WIDTH_A), BF16),
        scratch_shapes=[pltpu.VMEM((3, DEINT, s // DEINT, HEAD_DIM), F32)]
        + [pltpu.VMEM((DEINT, s // DEINT, HEAD_DIM), F32)] * 3
        + [pltpu.VMEM((s, HEAD_DIM), F32), pltpu.VMEM((4, HEAD_DIM, 2 * HEAD_DIM), F32)],
        compiler_params=_cparams("parallel", "parallel"),
        name="dilated_attn",
    )(view, view, view)
    return o.reshape(b * s, WIDTH_A)


def _sb_kernel(q_ref, k_ref, v_ref, o_ref, *, blk, nq, hg):
    row = lax.broadcasted_iota(jnp.int32, (blk, blk), 0)
    col = lax.broadcasted_iota(jnp.int32, (blk, blk), 1)
    tri = jnp.where(row > col, 1.0, 0.0).astype(BF16)
    causal = col < row
    heads = [slice(h * HEAD_DIM, (h + 1) * HEAD_DIM) for h in range(hg)]

    def qblock(i, carry):
        q0 = pl.multiple_of(i * blk, blk)
        qs = [q_ref[pl.ds(q0, blk), cs] for cs in heads]

        def step(st, diagonal):
            kb, cs_, accs = st
            k0 = pl.multiple_of(kb * blk, blk)
            logits, parts, new_c, new_acc = [], [], [], []
            for q, c, cs in zip(qs, cs_, heads):
                z = _dot_nt(q, k_ref[pl.ds(k0, blk), cs])
                sp = jnp.maximum(z, 0.0) + jnp.log2(1.0 + jnp.exp2(-jnp.abs(z)))
                lm = jnp.where(causal, -sp, 0.0) if diagonal else -sp
                hi = lm.astype(BF16)
                parts += [hi, (lm - hi.astype(F32)).astype(BF16)]
                logits.append(z - sp + c)
                new_c.append(c + jnp.sum(lm, axis=-1, keepdims=True))
            later = _dot(jnp.concatenate(parts, axis=0), tri)
            for h, (lg, acc, cs) in enumerate(zip(logits, accs, heads)):
                a = jnp.exp2(lg + later[2 * h * blk:(2 * h + 1) * blk] + later[(2 * h + 1) * blk:(2 * h + 2) * blk])
                if diagonal:
                    a = jnp.where(causal, a, 0.0)
                new_acc.append(acc + _dot(a.astype(BF16), v_ref[pl.ds(k0, blk), cs]))
            return kb - 1, tuple(new_c), tuple(new_acc)

        def cond(st):
            kb, cs_, _ = st
            live = jnp.max(functools.reduce(jnp.maximum, cs_)) > F32_EXP_ZERO * LOG2E
            return jnp.logical_and(kb >= 0, live)

        init = (i, tuple(jnp.zeros((blk, 1), F32) for _ in heads),
                tuple(jnp.zeros((blk, HEAD_DIM), F32) for _ in heads))
        _, _, accs = lax.while_loop(cond, lambda st: step(st, False), step(init, True))
        for acc, cs in zip(accs, heads):
            o_ref[pl.ds(q0, blk), cs] = acc.astype(BF16)
        return carry

    lax.fori_loop(0, nq, qblock, 0)


def _sb_attn(pb, b, s):
    view = pb.reshape(b, s, pb.shape[1])
    blk = min(s, 256)
    hg = 4
    cw = hg * HEAD_DIM
    ngrp = N_HEADS_B // hg

    def in_map(x):
        return lambda bi, g: (bi, 0, x * ngrp + g)

    o = pl.pallas_call(
        functools.partial(_sb_kernel, blk=blk, nq=s // blk, hg=hg),
        grid=(b, ngrp),
        in_specs=[pl.BlockSpec((None, s, cw), in_map(0)),
                  pl.BlockSpec((None, s, cw), in_map(1)),
                  pl.BlockSpec((None, s, cw), in_map(2))],
        out_specs=pl.BlockSpec((None, s, cw), lambda bi, g: (bi, 0, g)),
        out_shape=jax.ShapeDtypeStruct((b, s, WIDTH_B), BF16),
        compiler_params=_cparams("parallel", "parallel"),
        name="stick_breaking",
    )(view, view, view)
    return o.reshape(b * s, WIDTH_B)


def _mix_out_kernel(oa_ref, ob_ref, ga_ref, gb_ref, w_ref, h_ref, out_ref):
    na = _rms(oa_ref[...].astype(F32), ga_ref[...]).astype(BF16)
    nb = _rms(ob_ref[...].astype(F32), gb_ref[...]).astype(BF16)
    y = _dot(na, w_ref[:WIDTH_A, :]) + _dot(nb, w_ref[WIDTH_A:, :])
    out_ref[...] = h_ref[...] + y


def _mix_out(oa, ob, ga, gb, w_bf, h2):
    t, d = h2.shape
    tm = min(t, 512)
    ra = pl.BlockSpec((tm, WIDTH_A), lambda i: (i, 0))
    rd = pl.BlockSpec((tm, d), lambda i: (i, 0))
    return pl.pallas_call(
        _mix_out_kernel,
        grid=(t // tm,),
        in_specs=[ra, ra,
                  pl.BlockSpec((1, WIDTH_A), lambda i: (0, 0)),
                  pl.BlockSpec((1, WIDTH_B), lambda i: (0, 0)),
                  pl.BlockSpec(w_bf.shape, lambda i: (0, 0)),
                  rd],
        out_specs=rd,
        out_shape=jax.ShapeDtypeStruct((t, d), F32),
        compiler_params=_cparams("parallel"),
        name="mix_out_proj",
    )(oa, ob, ga.reshape(1, -1), gb.reshape(1, -1), w_bf, h2)


def _peer_scores_kernel(h_ref, g_ref, wq_ref, sub_ref, xt_ref, scnh_ref, s2_ref):
    nh, nk = PEER_HEADS, PEER_N_KEYS
    xn = _rms(h_ref[...], g_ref[...])
    xt_ref[...] = xn.T.astype(BF16)
    qr = _dot(xn.astype(BF16), wq_ref[...]).astype(BF16)
    for h in range(nh):
        for c in range(2):
            col = (h * 2 + c) * nk
            sc = _dot_nt(sub_ref[h, c], qr[:, col:col + nk])
            for lt in range(s2_ref.shape[0]):
                sl = sc[:, lt * LANES:(lt + 1) * LANES]
                scnh_ref[lt, pl.ds(c * nk * nh + h, nk, stride=nh), :] = sl
                if c == 1:
                    s2_ref[lt, h * nk:(h + 1) * nk, :] = sl


def _peer_scores(h2, g, wq_bf, sub_bf):
    t, d = h2.shape
    nh, _, nk, _ = sub_bf.shape
    tm = min(t, 512)
    return pl.pallas_call(
        _peer_scores_kernel,
        grid=(t // tm,),
        in_specs=[pl.BlockSpec((tm, d), lambda i: (i, 0)),
                  pl.BlockSpec((1, d), lambda i: (0, 0)),
                  pl.BlockSpec(wq_bf.shape, lambda i: (0, 0)),
                  pl.BlockSpec(sub_bf.shape, lambda i: (0, 0, 0, 0))],
        out_specs=[pl.BlockSpec((d, tm), lambda i: (0, i)),
                   pl.BlockSpec((tm // LANES, 2 * nk * nh, LANES), lambda i: (i, 0, 0)),
                   pl.BlockSpec((tm // LANES, nh * nk, LANES), lambda i: (i, 0, 0))],
        out_shape=[jax.ShapeDtypeStruct((d, t), BF16),
                   jax.ShapeDtypeStruct((t // LANES, 2 * nk * nh, LANES), F32),
                   jax.ShapeDtypeStruct((t // LANES, nh * nk, LANES), F32)],
        compiler_params=_cparams("parallel"),
        name="peer_scores",
    )(h2, g.reshape(1, d), wq_bf, sub_bf)


def _oddeven_merge_sort_pairs(n):
    pairs = []

    def merge(lo, m, r):
        step = r * 2
        if step < m:
            merge(lo, m, step)
            merge(lo + r, m, step)
            for i in range(lo + r, lo + m - r, step):
                pairs.append((i, i + r))
        else:
            pairs.append((lo, lo + r))

    def sort(lo, m):
        if m > 1:
            h = m // 2
            sort(lo, h)
            sort(lo + h, h)
            merge(lo, m, 1)

    sort(0, n)
    return pairs


_SORT16 = _oddeven_merge_sort_pairs(PEER_TOPK)


def _sort_desc(vals):
    vals = list(vals)
    for i, j in _SORT16:
        a, b = vals[i], vals[j]
        vals[i], vals[j] = jnp.maximum(a, b), jnp.minimum(a, b)
    return vals


def _merge_top(a, b):
    n = len(a)
    c = [jnp.maximum(a[i], b[n - 1 - i]) for i in range(n)]
    stride = n // 2
    while stride >= 1:
        for i in range(n):
            if (i // stride) % 2 == 0:
                x, y = c[i], c[i + stride]
                c[i], c[i + stride] = jnp.maximum(x, y), jnp.minimum(x, y)
        stride //= 2
    return c


def _top_sorted(vals):
    k = PEER_TOPK
    groups = [_sort_desc(vals[i:i + k]) for i in range(0, len(vals), k)]
    while len(groups) > 1:
        groups = [_merge_top(groups[i], groups[i + 1]) for i in range(0, len(groups), 2)]
    return groups[0]


def _peer_select_kernel(scnh_ref, s2_ref, c1_ref, g1_ref, e2_ref, *, nchunk):
    nk, nh, k = PEER_N_KEYS, PEER_HEADS, PEER_TOPK
    big = -NEG_INF
    for lc in range(nchunk):
        x1 = [scnh_ref[lc, n * nh:(n + 1) * nh, :] for n in range(nk)]
        x2 = [scnh_ref[lc, (nk + n) * nh:(nk + n + 1) * nh, :] for n in range(nk)]
        a = _top_sorted(x1)
        b = _top_sorted(x2)
        cand = [a[i] + b[j] for i in range(k) for j in range(k) if (i + 1) * (j + 1) <= k]
        pad = [jnp.full_like(a[0], NEG_INF)] * (-len(cand) % k)
        top = _top_sorted(cand + pad)
        c16 = top[k - 1]
        c17 = functools.reduce(jnp.maximum, [jnp.where(c < c16, c, NEG_INF) for c in cand])
        thr = 0.5 * (c16 + c17)
        mx = a[0] + b[0]
        z = functools.reduce(lambda u, v: u + v, [jnp.where(c > thr, jnp.exp(c - mx), 0.0) for c in cand])
        inv_z = 1.0 / z
        for n in range(nk):
            keep = x1[n] >= a[k - 1]
            c1_ref[lc, n * nh:(n + 1) * nh, :] = jnp.where(keep, jnp.exp(thr - x1[n] - b[0]) * inv_z, big)
            g1_ref[lc, n * nh:(n + 1) * nh, :] = jnp.exp(x1[n] - a[0])
        for h in range(nh):
            s2 = s2_ref[lc, h * nk:(h + 1) * nk, :]
            e = jnp.exp(s2 - b[0][h:h + 1, :]) * inv_z[h:h + 1, :]
            e2_ref[lc, h * nk:(h + 1) * nk, :] = jnp.where(s2 >= b[k - 1][h:h + 1, :], e, 0.0)


def _peer_select(scnh, s2):
    nt, rows, _ = s2.shape
    nchunk = min(nt, 2)
    spec = pl.BlockSpec((nchunk, rows, LANES), lambda i: (i, 0, 0))
    return pl.pallas_call(
        functools.partial(_peer_select_kernel, nchunk=nchunk),
        grid=(nt // nchunk,),
        in_specs=[pl.BlockSpec((nchunk, scnh.shape[1], LANES), lambda i: (i, 0, 0)), spec],
        out_specs=[spec, spec, spec],
        out_shape=[jax.ShapeDtypeStruct(s2.shape, F32)] * 3,
        compiler_params=_cparams("parallel"),
        name="peer_select",
    )(scnh, s2)


def _gelu(x):
    return 0.5 * x * (1.0 + lax.erf(x * (2.0 ** -0.5)))


def _peer_mix_kernel(xt_ref, u_ref, vt_ref, e2_ref, c1_ref, g1_ref, out_ref, hid_ref, a_ref, *, te, tm, cw):
    nk, nh = PEER_N_KEYS, PEER_HEADS
    nslab = te // nk
    jrows = nk // 2
    tiles = cw // LANES

    @pl.when(pl.program_id(1) == 0)
    def _():
        out_ref[...] = jnp.zeros_like(out_ref)
        hid_ref[...] = jnp.zeros_like(hid_ref)

    for c in range(tm // cw):
        for l in range(tiles):
            lt = c * tiles + l
            for jh in range(2):
                w = [jnp.zeros((jrows, LANES), F32) for _ in range(nslab)]
                for h in range(nh):
                    r0 = h * nk + jh * jrows
                    e2 = e2_ref[lt, r0:r0 + jrows, :]
                    for ii in range(nslab):
                        r = ii * nh + h
                        sel = e2 >= c1_ref[lt, r:r + 1, :]
                        w[ii] = w[ii] + jnp.where(sel, e2 * g1_ref[lt, r:r + 1, :], 0.0)
                for ii in range(nslab):
                    r0 = ii * nk + jh * jrows
                    a = _gelu(hid_ref[lt, r0:r0 + jrows, :]) * w[ii]
                    a_ref[r0:r0 + jrows, lt * LANES:(lt + 1) * LANES] = a.astype(BF16)
        out_ref[:, c * cw:(c + 1) * cw] += _dot(vt_ref[...], a_ref[:, c * cw:(c + 1) * cw])
        hid = _dot(u_ref[...], xt_ref[:, c * cw:(c + 1) * cw])
        for l in range(tiles):
            hid_ref[c * tiles + l] = hid[:, l * LANES:(l + 1) * LANES]


def _peer_mix(xt, u_bf, vt_bf, e2, c1, g1):
    d, t = xt.shape
    n_exp = u_bf.shape[0]
    nt, rows, _ = e2.shape
    tm = min(t, 1024)
    te = 512
    cw = min(tm, 256)
    n = n_exp // te
    crow = te // PEER_N_KEYS * PEER_HEADS
    once = pl.Buffered(1)
    tok = pl.BlockSpec((tm // LANES, rows, LANES), lambda i, s: (i, 0, 0), pipeline_mode=once)
    sel = pl.BlockSpec((tm // LANES, crow, LANES), lambda i, s: (i, jnp.maximum(s - 1, 0), 0))
    return pl.pallas_call(
        functools.partial(_peer_mix_kernel, te=te, tm=tm, cw=cw),
        grid=(t // tm, n + 1),
        in_specs=[pl.BlockSpec((d, tm), lambda i, s: (0, i), pipeline_mode=once),
                  pl.BlockSpec((te, d), lambda i, s: (jnp.minimum(s, n - 1), 0)),
                  pl.BlockSpec((d, te), lambda i, s: (0, jnp.maximum(s - 1, 0))),
                  tok, sel, sel],
        out_specs=pl.BlockSpec((d, tm), lambda i, s: (0, i)),
        out_shape=jax.ShapeDtypeStruct((d, t), F32),
        scratch_shapes=[pltpu.VMEM((tm // LANES, te, LANES), F32), pltpu.VMEM((te, tm), BF16)],
        compiler_params=_cparams("parallel", "arbitrary"),
        name="peer_mix",
    )(xt, u_bf, vt_bf, e2, c1, g1)


def _ple_kernel(h_ref, pt_ref, g_ref, wg_ref, p_ref, wp_ref, fn_ref, out_ref, *, final):
    h = h_ref[...] + pt_ref[...].T
    hn = _rms(h, g_ref[...]).astype(BF16)
    gate = jax.nn.sigmoid(_dot(hn, wg_ref[...]))
    emb = _dot(p_ref[...].astype(BF16), wp_ref[...])
    y = h + emb * gate
    if final:
        y = _rms(y, fn_ref[...])
    out_ref[...] = y


def _ple(h2, peer_t, g, wg_bf, p2, wp_bf, fn, final):
    t, d = h2.shape
    tm = min(t, 256)
    rd = pl.BlockSpec((tm, d), lambda i: (i, 0))
    one = pl.BlockSpec((1, d), lambda i: (0, 0))
    full = lambda a: pl.BlockSpec(a.shape, lambda i: (0, 0))
    return pl.pallas_call(
        functools.partial(_ple_kernel, final=final),
        grid=(t // tm,),
        in_specs=[rd, pl.BlockSpec((d, tm), lambda i: (0, i)), one, full(wg_bf),
                  pl.BlockSpec((tm, p2.shape[1]), lambda i: (i, 0)), full(wp_bf), one],
        out_specs=rd,
        out_shape=jax.ShapeDtypeStruct((t, d), F32),
        compiler_params=_cparams("parallel"),
        name="ple_final" if final else "ple",
    )(h2, peer_t, g.reshape(1, d), wg_bf, p2, wp_bf, fn.reshape(1, d))


def kernel(x, p, positions, attn_norm, w_in, out_norm_a, out_norm_b, w_out, ffn_norm, peer_wq, peer_subkeys,
           peer_u, peer_v, ple_norm, ple_gate, ple_proj, final_norm):
    b, s, d = x.shape
    depth = w_in.shape[0]
    t = b * s
    assert s % (LANES * max(dil for _, dil in DILATED_PATTERNS)) == 0
    assert all(win // dil == HEAD_DIM for win, dil in DILATED_PATTERNS)
    cos, sin = _rope_tables(positions)
    h = x.reshape(t, d)
    for i in range(depth):
        pa, pb = _in_proj(h, attn_norm[i], w_in[i].astype(BF16), cos, sin)
        oa = _dilated_attn(pa, b, s)
        ob = _sb_attn(pb, b, s)
        h = _mix_out(oa, ob, out_norm_a[i], out_norm_b[i], w_out[i].astype(BF16), h)
        xt, scnh, s2 = _peer_scores(h, ffn_norm[i], peer_wq[i].astype(BF16), peer_subkeys[i].astype(BF16))
        c1, g1, e2 = _peer_select(scnh, s2)
        peer_t = _peer_mix(xt, peer_u[i].astype(BF16), peer_v[i].T.astype(BF16), e2, c1, g1)
        h = _ple(h, peer_t, ple_norm[i], ple_gate[i].astype(BF16), p[i].reshape(t, -1),
                 ple_proj[i].astype(BF16), final_norm, final=(i == depth - 1))
    return h.reshape(b, s, d)
```

```python
import functools
import math

import jax
import jax.numpy as jnp
from jax import lax
from jax.experimental import pallas as pl
from jax.experimental.pallas import tpu as pltpu

F32 = jnp.float32
BF16 = jnp.bfloat16

HEAD_DIM = 128
N_HEADS_A = 8
N_HEADS_B = 8
WIDTH_A = N_HEADS_A * HEAD_DIM
WIDTH_B = N_HEADS_B * HEAD_DIM
DILATED_PATTERNS = ((128, 1), (512, 4), (2048, 16))
ROPE_THETA = 10000.0
PEER_HEADS = 8
PEER_N_KEYS = 128
PEER_TOPK = 16
NORM_EPS = 1e-6
NEG_INF = -1e30

LANES = 128
SUBLANES = 8
VMEM_LIMIT = 56 * 1024 * 1024
F32_EXP_ZERO = -104.0
LOG2E = math.log2(math.e)


def _cparams(*sem):
    return pltpu.CompilerParams(dimension_semantics=sem, vmem_limit_bytes=VMEM_LIMIT)


def _dot(a, b):
    return jnp.dot(a, b, preferred_element_type=F32)


def _dot_nt(a, b):
    return lax.dot_general(a, b, (((1,), (1,)), ((), ())), preferred_element_type=F32)


def _rms(x, g):
    return x * lax.rsqrt(jnp.mean(x * x, axis=-1, keepdims=True) + NORM_EPS) * g


def _rope_kernel(pos_ref, freq_ref, sign_ref, cos_ref, sin_ref):
    ang = pos_ref[...].astype(F32) * freq_ref[...]
    cos_ref[...] = jnp.cos(ang)
    sin_ref[...] = jnp.sin(ang) * sign_ref[...]


def _rope_tables(positions):
    t = positions.size
    half = HEAD_DIM // 2
    inv_freq = ROPE_THETA ** (-jnp.arange(half, dtype=F32) / half)
    freq = jnp.concatenate([inv_freq, inv_freq]).reshape(1, HEAD_DIM)
    sign = jnp.concatenate([-jnp.ones((half,), F32), jnp.ones((half,), F32)]).reshape(1, HEAD_DIM)
    tm = min(t, 1024)
    row = pl.BlockSpec((tm, HEAD_DIM), lambda i: (i, 0))
    one = pl.BlockSpec((1, HEAD_DIM), lambda i: (0, 0))
    return pl.pallas_call(
        _rope_kernel,
        grid=(t // tm,),
        in_specs=[pl.BlockSpec((tm, 1), lambda i: (i, 0)), one, one],
        out_specs=[row, row],
        out_shape=[jax.ShapeDtypeStruct((t, HEAD_DIM), F32)] * 2,
        compiler_params=_cparams("parallel"),
        name="rope_tables",
    )(positions.reshape(t, 1), freq, sign)


def _in_proj_kernel(x_ref, g_ref, w_ref, cos_ref, sin_ref, pa_ref, pb_ref, xn_ref, *, scale):
    j = pl.program_id(1)

    @pl.when(j == 0)
    def _():
        xn_ref[...] = _rms(x_ref[...], g_ref[...]).astype(BF16)

    sca = jnp.where(j == 0, scale, 1.0).astype(F32)
    c = jnp.where(j < 2, cos_ref[...] * sca, 1.0)
    s = jnp.where(j < 2, sin_ref[...] * sca, 0.0)
    scb = jnp.where(j == 3, scale * LOG2E, 1.0).astype(F32)
    cw = 2 * HEAD_DIM
    for n0 in range(0, w_ref.shape[1], cw):
        acc = _dot(xn_ref[...], w_ref[:, n0:n0 + cw])
        for cs in (slice(0, HEAD_DIM), slice(HEAD_DIM, cw)):
            a = acc[:, cs]
            pa_ref[:, n0 + cs.start:n0 + cs.stop] = a * c + pltpu.roll(a, HEAD_DIM // 2, 1) * s
        pb_ref[:, n0:n0 + cw] = (acc * scb).astype(BF16)


def _in_proj(h2, g, w_bf, cos, sin):
    t, d = h2.shape
    n = w_bf.shape[1]
    tm = min(t, 1024)
    tn = WIDTH_A
    na = 3
    return pl.pallas_call(
        functools.partial(_in_proj_kernel, scale=HEAD_DIM ** -0.5),
        grid=(t // tm, n // tn),
        in_specs=[
            pl.BlockSpec((tm, d), lambda i, j: (i, 0)),
            pl.BlockSpec((1, d), lambda i, j: (0, 0)),
            pl.BlockSpec((d, tn), lambda i, j: (0, j)),
            pl.BlockSpec((tm, HEAD_DIM), lambda i, j: (i, 0)),
            pl.BlockSpec((tm, HEAD_DIM), lambda i, j: (i, 0)),
        ],
        out_specs=[pl.BlockSpec((tm, tn), lambda i, j: (i, jnp.minimum(j, na))),
                   pl.BlockSpec((tm, tn), lambda i, j: (i, jnp.where(j < na, na, j - na)))],
        out_shape=[jax.ShapeDtypeStruct((t, (na + 1) * tn), F32),
                   jax.ShapeDtypeStruct((t, n - na * tn + tn), BF16)],
        scratch_shapes=[pltpu.VMEM((tm, d), BF16)],
        compiler_params=_cparams("parallel", "arbitrary"),
        name="in_proj",
    )(h2, g.reshape(1, d), w_bf, cos, sin)


DEINT = 4


def _dilated_kernel(q_ref, k_ref, v_ref, o_ref, x4, num, den, mac, onat, bias_ref, *, seq):
    w = HEAD_DIM
    nr = DEINT
    cq = w // nr
    qi = lax.broadcasted_iota(jnp.int32, (w, 2 * w), 0)
    kj = lax.broadcasted_iota(jnp.int32, (w, 2 * w), 1)
    sh = cq.bit_length() - 1
    qpos = (qi, nr * (qi & (cq - 1)) + (qi >> sh))
    kpos = (kj, nr * (kj & (2 * cq - 1)) + (kj >> (sh + 1)))
    for order in range(2):
        for later_block, off in ((0, 0), (1, w)):
            delta = qpos[order] - kpos[order] + (off - w // 2)
            bias_ref[2 * order + later_block] = jnp.where(jnp.abs(delta) <= w // 2, 0.0, NEG_INF)
    ones = jnp.ones((2 * w, w), BF16)

    def deinterleave(c, carry):
        src = pl.ds(pl.multiple_of(c * (nr * w), nr * w), nr * w)
        dst = pl.ds(pl.multiple_of(c * w, w), w)
        for t, ref in enumerate((q_ref, k_ref, v_ref)):
            for r in range(nr):
                x4[t, r, dst, :] = ref.at[src][pl.ds(r, w, stride=nr), :]
        return carry

    lax.fori_loop(0, seq // (nr * w), deinterleave, 0)

    def attend(q, k2, v2, bias, acc, first):
        s = _dot_nt(q.astype(BF16), k2.astype(BF16)) + bias
        m = jnp.max(s, axis=-1, keepdims=True)
        e = jnp.exp(s - m).astype(BF16)
        pv = _dot(e, jnp.concatenate([v2.astype(BF16), ones], axis=1))
        (ld_n, st_n), (ld_d, st_d), (ld_m, st_m) = acc
        if first:
            st_n(pv[:, :w])
            st_d(pv[:, w:])
            st_m(jnp.broadcast_to(m, (w, w)))
        else:
            m0 = ld_m()
            mn = jnp.maximum(m0, m)
            a0 = jnp.exp(m0 - mn)
            a1 = jnp.exp(m - mn)
            st_n(a0 * ld_n() + a1 * pv[:, :w])
            st_d(a0 * ld_d() + a1 * pv[:, w:])
            st_m(mn)

    def block_d1(b):
        pb = jnp.maximum(b - 1, 0)
        qrow = pl.ds(pl.multiple_of(b * cq, cq), cq)
        krow = pl.ds(pl.multiple_of(pb * cq, cq), 2 * cq)
        q = jnp.concatenate([x4[0, r, qrow, :] for r in range(nr)], axis=0)
        k2 = jnp.concatenate([x4[1, r, krow, :] for r in range(nr)], axis=0)
        v2 = jnp.concatenate([x4[2, r, krow, :] for r in range(nr)], axis=0)

        def chunked(ref):
            def store(val):
                for r in range(nr):
                    ref[r, qrow, :] = val[r * cq:(r + 1) * cq]
            return (lambda: jnp.concatenate([ref[r, qrow, :] for r in range(nr)], axis=0)), store

        attend(q, k2, v2, bias_ref[2 + b - pb], [chunked(num), chunked(den), chunked(mac)], True)

    def block_dn(blk, r, sub, nsub):
        span = w * nsub
        pblk = jnp.maximum(blk - 1, 0)
        qwin = pl.ds(pl.multiple_of(blk * span, span), span)
        kwin = pl.ds(pl.multiple_of(pblk * span, span), 2 * span)
        rows = (lambda n: pl.ds(sub, n, stride=nsub)) if nsub > 1 else (lambda n: pl.ds(0, n))

        def strided(ref):
            view = ref.at[r, qwin]
            def store(val):
                view[rows(w), :] = val
            return (lambda: view[rows(w), :]), store

        attend(x4.at[0, r, qwin][rows(w), :], x4.at[1, r, kwin][rows(2 * w), :], x4.at[2, r, kwin][rows(2 * w), :],
               bias_ref[blk - pblk], [strided(num), strided(den), strided(mac)], False)

    for _, d in DILATED_PATTERNS:
        nblk = seq // d // w
        if d == 1:
            ub = 8

            def body(g, carry, ub=ub):
                for u in range(ub):
                    block_d1(g * ub + u)
                return carry
        else:
            nsub = d // nr
            ub = max(1, 8 // d)

            def body(g, carry, ub=ub, nsub=nsub):
                for u in range(ub):
                    for r in range(nr):
                        for sub in range(nsub):
                            block_dn(g * ub + u, r, sub, nsub)
                return carry

        lax.fori_loop(0, nblk // ub, body, 0)

    def interleave(c, carry):
        src = pl.ds(pl.multiple_of(c * w, w), w)
        dst = pl.ds(pl.multiple_of(c * (nr * w), nr * w), nr * w)
        for r in range(nr):
            onat.at[dst][pl.ds(r, w, stride=nr), :] = num[r, src, :] / den[r, src, :]
        return carry

    lax.fori_loop(0, seq // (nr * w), interleave, 0)
    o_ref[...] = onat[...].astype(BF16)


def _dilated_attn(pa, b, s):
    view = pa.reshape(b, s, pa.shape[1])

    def in_map(x):
        return lambda bi, h: (bi, 0, x * N_HEADS_A + h)

    spec = lambda x: pl.BlockSpec((None, s, HEAD_DIM), in_map(x))
    o = pl.pallas_call(
        functools.partial(_dilated_kernel, seq=s),
        grid=(b, N_HEADS_A),
        in_specs=[spec(0), spec(1), spec(2)],
        out_specs=pl.BlockSpec((None, s, HEAD_DIM), lambda bi, h: (bi, 0, h)),
        out_shape=jax.ShapeDtypeStruct((b, s, WIDTH_A), BF16),
        scratch_shapes=[pltpu.VMEM((3, DEINT, s // DEINT, HEAD_DIM), F32)]
        + [pltpu.VMEM((DEINT, s // DEINT, HEAD_DIM), F32)] * 3
        + [pltpu.VMEM((s, HEAD_DIM), F32), pltpu.VMEM((4, HEAD_DIM, 2 * HEAD_DIM), F32)],
        compiler_params=_cparams("parallel", "parallel"),
        name="dilated_attn",
    )(view, view, view)
    return o.reshape(b * s, WIDTH_A)


def _sb_kernel(q_ref, k_ref, v_ref, o_ref, *, blk, nq, hg):
    row = lax.broadcasted_iota(jnp.int32, (blk, blk), 0)
    col = lax.broadcasted_iota(jnp.int32, (blk, blk), 1)
    tri = jnp.where(row > col, 1.0, 0.0).astype(BF16)
    causal = col < row
    heads = [slice(h * HEAD_DIM, (h + 1) * HEAD_DIM) for h in range(hg)]

    def qblock(i, carry):
        q0 = pl.multiple_of(i * blk, blk)
        qs = [q_ref[pl.ds(q0, blk), cs] for cs in heads]

        def step(st, diagonal):
            kb, cs_, accs = st
            k0 = pl.multiple_of(kb * blk, blk)
            logits, parts, new_c, new_acc = [], [], [], []
            for q, c, cs in zip(qs, cs_, heads):
                z = _dot_nt(q, k_ref[pl.ds(k0, blk), cs])
                sp = jnp.maximum(z, 0.0) + jnp.log2(1.0 + jnp.exp2(-jnp.abs(z)))
                lm = jnp.where(causal, -sp, 0.0) if diagonal else -sp
                hi = lm.astype(BF16)
                parts += [hi, (lm - hi.astype(F32)).astype(BF16)]
                logits.append(z - sp + c)
                new_c.append(c + jnp.sum(lm, axis=-1, keepdims=True))
            later = _dot(jnp.concatenate(parts, axis=0), tri)
            for h, (lg, acc, cs) in enumerate(zip(logits, accs, heads)):
                a = jnp.exp2(lg + later[2 * h * blk:(2 * h + 1) * blk] + later[(2 * h + 1) * blk:(2 * h + 2) * blk])
                if diagonal:
                    a = jnp.where(causal, a, 0.0)
                new_acc.append(acc + _dot(a.astype(BF16), v_ref[pl.ds(k0, blk), cs]))
            return kb - 1, tuple(new_c), tuple(new_acc)

        def cond(st):
            kb, cs_, _ = st
            live = jnp.max(functools.reduce(jnp.maximum, cs_)) > F32_EXP_ZERO * LOG2E
            return jnp.logical_and(kb >= 0, live)

        init = (i, tuple(jnp.zeros((blk, 1), F32) for _ in heads),
                tuple(jnp.zeros((blk, HEAD_DIM), F32) for _ in heads))
        _, _, accs = lax.while_loop(cond, lambda st: step(st, False), step(init, True))
        for acc, cs in zip(accs, heads):
            o_ref[pl.ds(q0, blk), cs] = acc.astype(BF16)
        return carry

    lax.fori_loop(0, nq, qblock, 0)


def _sb_attn(pb, b, s):
    view = pb.reshape(b, s, pb.shape[1])
    blk = min(s, 256)
    hg = 4
    cw = hg * HEAD_DIM
    ngrp = N_HEADS_B // hg

    def in_map(x):
        return lambda bi, g: (bi, 0, x * ngrp + g)

    o = pl.pallas_call(
        functools.partial(_sb_kernel, blk=blk, nq=s // blk, hg=hg),
        grid=(b, ngrp),
        in_specs=[pl.BlockSpec((None, s, cw), in_map(0)),
                  pl.BlockSpec((None, s, cw), in_map(1)),
                  pl.BlockSpec((None, s, cw), in_map(2))],
        out_specs=pl.BlockSpec((None, s, cw), lambda bi, g: (bi, 0, g)),
        out_shape=jax.ShapeDtypeStruct((b, s, WIDTH_B), BF16),
        compiler_params=_cparams("parallel", "parallel"),
        name="stick_breaking",
    )(view, view, view)
    return o.reshape(b * s, WIDTH_B)


def _mix_out_kernel(oa_ref, ob_ref, ga_ref, gb_ref, w_ref, h_ref, out_ref):
    na = _rms(oa_ref[...].astype(F32), ga_ref[...]).astype(BF16)
    nb = _rms(ob_ref[...].astype(F32), gb_ref[...]).astype(BF16)
    y = _dot(na, w_ref[:WIDTH_A, :]) + _dot(nb, w_ref[WIDTH_A:, :])
    out_ref[...] = h_ref[...] + y


def _mix_out(oa, ob, ga, gb, w_bf, h2):
    t, d = h2.shape
    tm = min(t, 512)
    ra = pl.BlockSpec((tm, WIDTH_A), lambda i: (i, 0))
    rd = pl.BlockSpec((tm, d), lambda i: (i, 0))
    return pl.pallas_call(
        _mix_out_kernel,
        grid=(t // tm,),
        in_specs=[ra, ra,
                  pl.BlockSpec((1, WIDTH_A), lambda i: (0, 0)),
                  pl.BlockSpec((1, WIDTH_B), lambda i: (0, 0)),
                  pl.BlockSpec(w_bf.shape, lambda i: (0, 0)),
                  rd],
        out_specs=rd,
        out_shape=jax.ShapeDtypeStruct((t, d), F32),
        compiler_params=_cparams("parallel"),
        name="mix_out_proj",
    )(oa, ob, ga.reshape(1, -1), gb.reshape(1, -1), w_bf, h2)


def _peer_scores_kernel(h_ref, g_ref, wq_ref, sub_ref, xt_ref, scnh_ref, s2_ref):
    nh, nk = PEER_HEADS, PEER_N_KEYS
    xn = _rms(h_ref[...], g_ref[...])
    xt_ref[...] = xn.T.astype(BF16)
    qr = _dot(xn.astype(BF16), wq_ref[...]).astype(BF16)
    for h in range(nh):
        for c in range(2):
            col = (h * 2 + c) * nk
            sc = _dot_nt(sub_ref[h, c], qr[:, col:col + nk])
            for lt in range(s2_ref.shape[0]):
                sl = sc[:, lt * LANES:(lt + 1) * LANES]
                scnh_ref[lt, pl.ds(c * nk * nh + h, nk, stride=nh), :] = sl
                if c == 1:
                    s2_ref[lt, h * nk:(h + 1) * nk, :] = sl


def _peer_scores(h2, g, wq_bf, sub_bf):
    t, d = h2.shape
    nh, _, nk, _ = sub_bf.shape
    tm = min(t, 512)
    return pl.pallas_call(
        _peer_scores_kernel,
        grid=(t // tm,),
        in_specs=[pl.BlockSpec((tm, d), lambda i: (i, 0)),
                  pl.BlockSpec((1, d), lambda i: (0, 0)),
                  pl.BlockSpec(wq_bf.shape, lambda i: (0, 0)),
                  pl.BlockSpec(sub_bf.shape, lambda i: (0, 0, 0, 0))],
        out_specs=[pl.BlockSpec((d, tm), lambda i: (0, i)),
                   pl.BlockSpec((tm // LANES, 2 * nk * nh, LANES), lambda i: (i, 0, 0)),
                   pl.BlockSpec((tm // LANES, nh * nk, LANES), lambda i: (i, 0, 0))],
        out_shape=[jax.ShapeDtypeStruct((d, t), BF16),
                   jax.ShapeDtypeStruct((t // LANES, 2 * nk * nh, LANES), F32),
                   jax.ShapeDtypeStruct((t // LANES, nh * nk, LANES), F32)],
        compiler_params=_cparams("parallel"),
        name="peer_scores",
    )(h2, g.reshape(1, d), wq_bf, sub_bf)


def _oddeven_merge_sort_pairs(n):
    pairs = []

    def merge(lo, m, r):
        step = r * 2
        if step < m:
            merge(lo, m, step)
            merge(lo + r, m, step)
            for i in range(lo + r, lo + m - r, step):
                pairs.append((i, i + r))
        else:
            pairs.append((lo, lo + r))

    def sort(lo, m):
        if m > 1:
            h = m // 2
            sort(lo, h)
            sort(lo + h, h)
            merge(lo, m, 1)

    sort(0, n)
    return pairs


_SORT16 = _oddeven_merge_sort_pairs(PEER_TOPK)


def _sort_desc(vals):
    vals = list(vals)
    for i, j in _SORT16:
        a, b = vals[i], vals[j]
        vals[i], vals[j] = jnp.maximum(a, b), jnp.minimum(a, b)
    return vals


def _merge_top(a, b):
    n = len(a)
    c = [jnp.maximum(a[i], b[n - 1 - i]) for i in range(n)]
    stride = n // 2
    while stride >= 1:
        for i in range(n):
            if (i // stride) % 2 == 0:
                x, y = c[i], c[i + stride]
                c[i], c[i + stride] = jnp.maximum(x, y), jnp.minimum(x, y)
        stride //= 2
    return c


def _top_sorted(vals):
    k = PEER_TOPK
    groups = [_sort_desc(vals[i:i + k]) for i in range(0, len(vals), k)]
    while len(groups) > 1:
        groups = [_merge_top(groups[i], groups[i + 1]) for i in range(0, len(groups), 2)]
    return groups[0]


def _peer_select_kernel(scnh_ref, s2_ref, c1_ref, g1_ref, e2_ref, *, nchunk):
    nk, nh, k = PEER_N_KEYS, PEER_HEADS, PEER_TOPK
    big = -NEG_INF
    for lc in range(nchunk):
        x1 = [scnh_ref[lc, n * nh:(n + 1) * nh, :] for n in range(nk)]
        x2 = [scnh_ref[lc, (nk + n) * nh:(nk + n + 1) * nh, :] for n in range(nk)]
        a = _top_sorted(x1)
        b = _top_sorted(x2)
        cand = [a[i] + b[j] for i in range(k) for j in range(k) if (i + 1) * (j + 1) <= k]
        pad = [jnp.full_like(a[0], NEG_INF)] * (-len(cand) % k)
        top = _top_sorted(cand + pad)
        c16 = top[k - 1]
        c17 = functools.reduce(jnp.maximum, [jnp.where(c < c16, c, NEG_INF) for c in cand])
        thr = 0.5 * (c16 + c17)
        mx = a[0] + b[0]
        z = functools.reduce(lambda u, v: u + v, [jnp.where(c > thr, jnp.exp(c - mx), 0.0) for c in cand])
        inv_z = 1.0 / z
        for n in range(nk):
            keep = x1[n] >= a[k - 1]
            c1_ref[lc, n * nh:(n + 1) * nh, :] = jnp.where(keep, jnp.exp(thr - x1[n] - b[0]) * inv_z, big)
            g1_ref[lc, n * nh:(n + 1) * nh, :] = jnp.exp(x1[n] - a[0])
        for h in range(nh):
            s2 = s2_ref[lc, h * nk:(h + 1) * nk, :]
            e = jnp.exp(s2 - b[0][h:h + 1, :]) * inv_z[h:h + 1, :]
            e2_ref[lc, h * nk:(h + 1) * nk, :] = jnp.where(s2 >= b[k - 1][h:h + 1, :], e, 0.0)


def _peer_select(scnh, s2):
    nt, rows, _ = s2.shape
    nchunk = min(nt, 2)
    spec = pl.BlockSpec((nchunk, rows, LANES), lambda i: (i, 0, 0))
    return pl.pallas_call(
        functools.partial(_peer_select_kernel, nchunk=nchunk),
        grid=(nt // nchunk,),
        in_specs=[pl.BlockSpec((nchunk, scnh.shape[1], LANES), lambda i: (i, 0, 0)), spec],
        out_specs=[spec, spec, spec],
        out_shape=[jax.ShapeDtypeStruct(s2.shape, F32)] * 3,
        compiler_params=_cparams("parallel"),
        name="peer_select",
    )(scnh, s2)


def _gelu(x):
    return 0.5 * x * (1.0 + lax.erf(x * (2.0 ** -0.5)))


def _peer_mix_kernel(xt_ref, u_ref, vt_ref, e2_ref, c1_ref, g1_ref, out_ref, hid_ref, a_ref, *, te, tm, cw):
    nk, nh = PEER_N_KEYS, PEER_HEADS
    nslab = te // nk
    jrows = nk // 2
    tiles = cw // LANES

    @pl.when(pl.program_id(1) == 0)
    def _():
        out_ref[...] = jnp.zeros_like(out_ref)
        hid_ref[...] = jnp.zeros_like(hid_ref)

    for c in range(tm // cw):
        for l in range(tiles):
            lt = c * tiles + l
            for jh in range(2):
                w = [jnp.zeros((jrows, LANES), F32) for _ in range(nslab)]
                for h in range(nh):
                    r0 = h * nk + jh * jrows
                    e2 = e2_ref[lt, r0:r0 + jrows, :]
                    for ii in range(nslab):
                        r = ii * nh + h
                        sel = e2 >= c1_ref[lt, r:r + 1, :]
                        w[ii] = w[ii] + jnp.where(sel, e2 * g1_ref[lt, r:r + 1, :], 0.0)
                for ii in range(nslab):
                    r0 = ii * nk + jh * jrows
                    a = _gelu(hid_ref[lt, r0:r0 + jrows, :]) * w[ii]
                    a_ref[r0:r0 + jrows, lt * LANES:(lt + 1) * LANES] = a.astype(BF16)
        out_ref[:, c * cw:(c + 1) * cw] += _dot(vt_ref[...], a_ref[:, c * cw:(c + 1) * cw])
        hid = _dot(u_ref[...], xt_ref[:, c * cw:(c + 1) * cw])
        for l in range(tiles):
            hid_ref[c * tiles + l] = hid[:, l * LANES:(l + 1) * LANES]


def _peer_mix(xt, u_bf, vt_bf, e2, c1, g1):
    d, t = xt.shape
    n_exp = u_bf.shape[0]
    nt, rows, _ = e2.shape
    tm = min(t, 1024)
    te = 512
    cw = min(tm, 256)
    n = n_exp // te
    crow = te // PEER_N_KEYS * PEER_HEADS
    once = pl.Buffered(1)
    tok = pl.BlockSpec((tm // LANES, rows, LANES), lambda i, s: (i, 0, 0), pipeline_mode=once)
    sel = pl.BlockSpec((tm // LANES, crow, LANES), lambda i, s: (i, jnp.maximum(s - 1, 0), 0))
    return pl.pallas_call(
        functools.partial(_peer_mix_kernel, te=te, tm=tm, cw=cw),
        grid=(t // tm, n + 1),
        in_specs=[pl.BlockSpec((d, tm), lambda i, s: (0, i), pipeline_mode=once),
                  pl.BlockSpec((te, d), lambda i, s: (jnp.minimum(s, n - 1), 0)),
                  pl.BlockSpec((d, te), lambda i, s: (0, jnp.maximum(s - 1, 0))),
                  tok, sel, sel],
        out_specs=pl.BlockSpec((d, tm), lambda i, s: (0, i)),
        out_shape=jax.ShapeDtypeStruct((d, t), F32),
        scratch_shapes=[pltpu.VMEM((tm // LANES, te, LANES), F32), pltpu.VMEM((te, tm), BF16)],
        compiler_params=_cparams("parallel", "arbitrary"),
        name="peer_mix",
    )(xt, u_bf, vt_bf, e2, c1, g1)


def _ple_kernel(h_ref, pt_ref, g_ref, wg_ref, p_ref, wp_ref, fn_ref, out_ref, *, final):
    h = h_ref[...] + pt_ref[...].T
    hn = _rms(h, g_ref[...]).astype(BF16)
    gate = jax.nn.sigmoid(_dot(hn, wg_ref[...]))
    emb = _dot(p_ref[...].astype(BF16), wp_ref[...])
    y = h + emb * gate
    if final:
        y = _rms(y, fn_ref[...])
    out_ref[...] = y


def _ple(h2, peer_t, g, wg_bf, p2, wp_bf, fn, final):
    t, d = h2.shape
    tm = min(t, 256)
    rd = pl.BlockSpec((tm, d), lambda i: (i, 0))
    one = pl.BlockSpec((1, d), lambda i: (0, 0))
    full = lambda a: pl.BlockSpec(a.shape, lambda i: (0, 0))
    return pl.pallas_call(
        functools.partial(_ple_kernel, final=final),
        grid=(t // tm,),
        in_specs=[rd, pl.BlockSpec((d, tm), lambda i: (0, i)), one, full(wg_bf),
                  pl.BlockSpec((tm, p2.shape[1]), lambda i: (i, 0)), full(wp_bf), one],
        out_specs=rd,
        out_shape=jax.ShapeDtypeStruct((t, d), F32),
        compiler_params=_cparams("parallel"),
        name="ple_final" if final else "ple",
    )(h2, peer_t, g.reshape(1, d), wg_bf, p2, wp_bf, fn.reshape(1, d))


def kernel(x, p, positions, attn_norm, w_in, out_norm_a, out_norm_b, w_out, ffn_norm, peer_wq, peer_subkeys,
           peer_u, peer_v, ple_norm, ple_gate, ple_proj, final_norm):
    b, s, d = x.shape
    depth = w_in.shape[0]
    t = b * s
    assert s % (LANES * max(dil for _, dil in DILATED_PATTERNS)) == 0
    assert all(win // dil == HEAD_DIM for win, dil in DILATED_PATTERNS)
    cos, sin = _rope_tables(positions)
    h = x.reshape(t, d)
    for i in range(depth):
        pa, pb = _in_proj(h, attn_norm[i], w_in[i].astype(BF16), cos, sin)
        oa = _dilated_attn(pa, b, s)
        ob = _sb_attn(pb, b, s)
        h = _mix_out(oa, ob, out_norm_a[i], out_norm_b[i], w_out[i].astype(BF16), h)
        xt, scnh, s2 = _peer_scores(h, ffn_norm[i], peer_wq[i].astype(BF16), peer_subkeys[i].astype(BF16))
        c1, g1, e2 = _peer_select(scnh, s2)
        peer_t = _peer_mix(xt, peer_u[i].astype(BF16), peer_v[i].T.astype(BF16), e2, c1, g1)
        h = _ple(h, peer_t, ple_norm[i], ple_gate[i].astype(BF16), p[i].reshape(t, -1),
                 ple_proj[i].astype(BF16), final_norm, final=(i == depth - 1))
    return h.reshape(b, s, d)
```

```python
import functools
import math

import jax
import jax.numpy as jnp
from jax import lax
from jax.experimental import pallas as pl
from jax.experimental.pallas import tpu as pltpu

F32 = jnp.float32
BF16 = jnp.bfloat16

HEAD_DIM = 128
N_HEADS_A = 8
N_HEADS_B = 8
WIDTH_A = N_HEADS_A * HEAD_DIM
WIDTH_B = N_HEADS_B * HEAD_DIM
DILATED_PATTERNS = ((128, 1), (512, 4), (2048, 16))
ROPE_THETA = 10000.0
PEER_HEADS = 8
PEER_N_KEYS = 128
PEER_TOPK = 16
NORM_EPS = 1e-6
NEG_INF = -1e30

LANES = 128
SUBLANES = 8
VMEM_LIMIT = 56 * 1024 * 1024
F32_EXP_ZERO = -104.0
LOG2E = math.log2(math.e)


def _cparams(*sem):
    return pltpu.CompilerParams(dimension_semantics=sem, vmem_limit_bytes=VMEM_LIMIT)


def _dot(a, b):
    return jnp.dot(a, b, preferred_element_type=F32)


def _dot_nt(a, b):
    return lax.dot_general(a, b, (((1,), (1,)), ((), ())), preferred_element_type=F32)


def _rms(x, g):
    return x * lax.rsqrt(jnp.mean(x * x, axis=-1, keepdims=True) + NORM_EPS) * g


def _rope_kernel(pos_ref, freq_ref, sign_ref, cos_ref, sin_ref):
    ang = pos_ref[...].astype(F32) * freq_ref[...]
    cos_ref[...] = jnp.cos(ang)
    sin_ref[...] = jnp.sin(ang) * sign_ref[...]


def _rope_tables(positions):
    t = positions.size
    half = HEAD_DIM // 2
    inv_freq = ROPE_THETA ** (-jnp.arange(half, dtype=F32) / half)
    freq = jnp.concatenate([inv_freq, inv_freq]).reshape(1, HEAD_DIM)
    sign = jnp.concatenate([-jnp.ones((half,), F32), jnp.ones((half,), F32)]).reshape(1, HEAD_DIM)
    tm = min(t, 1024)
    row = pl.BlockSpec((tm, HEAD_DIM), lambda i: (i, 0))
    one = pl.BlockSpec((1, HEAD_DIM), lambda i: (0, 0))
    return pl.pallas_call(
        _rope_kernel,
        grid=(t // tm,),
        in_specs=[pl.BlockSpec((tm, 1), lambda i: (i, 0)), one, one],
        out_specs=[row, row],
        out_shape=[jax.ShapeDtypeStruct((t, HEAD_DIM), F32)] * 2,
        compiler_params=_cparams("parallel"),
        name="rope_tables",
    )(positions.reshape(t, 1), freq, sign)


def _in_proj_kernel(x_ref, g_ref, w_ref, cos_ref, sin_ref, pa_ref, pb_ref, xn_ref, *, scale):
    j = pl.program_id(1)

    @pl.when(j == 0)
    def _():
        xn_ref[...] = _rms(x_ref[...], g_ref[...]).astype(BF16)

    sca = jnp.where(j == 0, scale, 1.0).astype(F32)
    c = jnp.where(j < 2, cos_ref[...] * sca, 1.0)
    s = jnp.where(j < 2, sin_ref[...] * sca, 0.0)
    scb = jnp.where(j == 3, scale * LOG2E, 1.0).astype(F32)
    cw = 2 * HEAD_DIM
    for n0 in range(0, w_ref.shape[1], cw):
        acc = _dot(xn_ref[...], w_ref[:, n0:n0 + cw])
        for cs in (slice(0, HEAD_DIM), slice(HEAD_DIM, cw)):
            a = acc[:, cs]
            pa_ref[:, n0 + cs.start:n0 + cs.stop] = a * c + pltpu.roll(a, HEAD_DIM // 2, 1) * s
        pb_ref[:, n0:n0 + cw] = (acc * scb).astype(BF16)


def _in_proj(h2, g, w_bf, cos, sin):
    t, d = h2.shape
    n = w_bf.shape[1]
    tm = min(t, 1024)
    tn = WIDTH_A
    na = 3
    return pl.pallas_call(
        functools.partial(_in_proj_kernel, scale=HEAD_DIM ** -0.5),
        grid=(t // tm, n // tn),
        in_specs=[
            pl.BlockSpec((tm, d), lambda i, j: (i, 0)),
            pl.BlockSpec((1, d), lambda i, j: (0, 0)),
            pl.BlockSpec((d, tn), lambda i, j: (0, j)),
            pl.BlockSpec((tm, HEAD_DIM), lambda i, j: (i, 0)),
            pl.BlockSpec((tm, HEAD_DIM), lambda i, j: (i, 0)),
        ],
        out_specs=[pl.BlockSpec((tm, tn), lambda i, j: (i, jnp.minimum(j, na))),
                   pl.BlockSpec((tm, tn), lambda i, j: (i, jnp.where(j < na, na, j - na)))],
        out_shape=[jax.ShapeDtypeStruct((t, (na + 1) * tn), F32),
                   jax.ShapeDtypeStruct((t, n - na * tn + tn), BF16)],
        scratch_shapes=[pltpu.VMEM((tm, d), BF16)],
        compiler_params=_cparams("parallel", "arbitrary"),
        name="in_proj",
    )(h2, g.reshape(1, d), w_bf, cos, sin)


DEINT = 4


def _dilated_kernel(q_ref, k_ref, v_ref, o_ref, x4, num, den, mac, onat, bias_ref, *, seq):
    w = HEAD_DIM
    nr = DEINT
    cq = w // nr
    qi = lax.broadcasted_iota(jnp.int32, (w, 2 * w), 0)
    kj = lax.broadcasted_iota(jnp.int32, (w, 2 * w), 1)
    sh = cq.bit_length() - 1
    qpos = (qi, nr * (qi & (cq - 1)) + (qi >> sh))
    kpos = (kj, nr * (kj & (2 * cq - 1)) + (kj >> (sh + 1)))
    for order in range(2):
        for later_block, off in ((0, 0), (1, w)):
            delta = qpos[order] - kpos[order] + (off - w // 2)
            bias_ref[2 * order + later_block] = jnp.where(jnp.abs(delta) <= w // 2, 0.0, NEG_INF)
    ones = jnp.ones((2 * w, w), BF16)

    def deinterleave(c, carry):
        src = pl.ds(pl.multiple_of(c * (nr * w), nr * w), nr * w)
        dst = pl.ds(pl.multiple_of(c * w, w), w)
        for t, ref in enumerate((q_ref, k_ref, v_ref)):
            for r in range(nr):
                x4[t, r, dst, :] = ref.at[src][pl.ds(r, w, stride=nr), :]
        return carry

    lax.fori_loop(0, seq // (nr * w), deinterleave, 0)

    def attend(q, k2, v2, bias, acc, first):
        s = _dot_nt(q.astype(BF16), k2.astype(BF16)) + bias
        m = jnp.max(s, axis=-1, keepdims=True)
        e = jnp.exp(s - m).astype(BF16)
        pv = _dot(e, jnp.concatenate([v2.astype(BF16), ones], axis=1))
        (ld_n, st_n), (ld_d, st_d), (ld_m, st_m) = acc
        if first:
            st_n(pv[:, :w])
            st_d(pv[:, w:])
            st_m(jnp.broadcast_to(m, (w, w)))
        else:
            m0 = ld_m()
            mn = jnp.maximum(m0, m)
            a0 = jnp.exp(m0 - mn)
            a1 = jnp.exp(m - mn)
            st_n(a0 * ld_n() + a1 * pv[:, :w])
            st_d(a0 * ld_d() + a1 * pv[:, w:])
            st_m(mn)

    def block_d1(b):
        pb = jnp.maximum(b - 1, 0)
        qrow = pl.ds(pl.multiple_of(b * cq, cq), cq)
        krow = pl.ds(pl.multiple_of(pb * cq, cq), 2 * cq)
        q = jnp.concatenate([x4[0, r, qrow, :] for r in range(nr)], axis=0)
        k2 = jnp.concatenate([x4[1, r, krow, :] for r in range(nr)], axis=0)
        v2 = jnp.concatenate([x4[2, r, krow, :] for r in range(nr)], axis=0)

        def chunked(ref):
            def store(val):
                for r in range(nr):
                    ref[r, qrow, :] = val[r * cq:(r + 1) * cq]
            return (lambda: jnp.concatenate([ref[r, qrow, :] for r in range(nr)], axis=0)), store

        attend(q, k2, v2, bias_ref[2 + b - pb], [chunked(num), chunked(den), chunked(mac)], True)

    def block_dn(blk, r, sub, nsub):
        span = w * nsub
        pblk = jnp.maximum(blk - 1, 0)
        qwin = pl.ds(pl.multiple_of(blk * span, span), span)
        kwin = pl.ds(pl.multiple_of(pblk * span, span), 2 * span)
        rows = (lambda n: pl.ds(sub, n, stride=nsub)) if nsub > 1 else (lambda n: pl.ds(0, n))

        def strided(ref):
            view = ref.at[r, qwin]
            def store(val):
                view[rows(w), :] = val
            return (lambda: view[rows(w), :]), store

        attend(x4.at[0, r, qwin][rows(w), :], x4.at[1, r, kwin][rows(2 * w), :], x4.at[2, r, kwin][rows(2 * w), :],
               bias_ref[blk - pblk], [strided(num), strided(den), strided(mac)], False)

    for _, d in DILATED_PATTERNS:
        nblk = seq // d // w
        if d == 1:
            ub = 8

            def body(g, carry, ub=ub):
                for u in range(ub):
                    block_d1(g * ub + u)
                return carry
        else:
            nsub = d // nr
            ub = max(1, 8 // d)

            def body(g, carry, ub=ub, nsub=nsub):
                for u in range(ub):
                    for r in range(nr):
                        for sub in range(nsub):
                            block_dn(g * ub + u, r, sub, nsub)
                return carry

        lax.fori_loop(0, nblk // ub, body, 0)

    def interleave(c, carry):
        src = pl.ds(pl.multiple_of(c * w, w), w)
        dst = pl.ds(pl.multiple_of(c * (nr * w), nr * w), nr * w)
        for r in range(nr):
            onat.at[dst][pl.ds(r, w, stride=nr), :] = num[r, src, :] / den[r, src, :]
        return carry

    lax.fori_loop(0, seq // (nr * w), interleave, 0)
    o_ref[...] = onat[...].astype(BF16)


def _dilated_attn(pa, b, s):
    view = pa.reshape(b, s, pa.shape[1])

    def in_map(x):
        return lambda bi, h: (bi, 0, x * N_HEADS_A + h)

    spec = lambda x: pl.BlockSpec((None, s, HEAD_DIM), in_map(x))
    o = pl.pallas_call(
        functools.partial(_dilated_kernel, seq=s),
        grid=(b, N_HEADS_A),
        in_specs=[spec(0), spec(1), spec(2)],
        out_specs=pl.BlockSpec((None, s, HEAD_DIM), lambda bi, h: (bi, 0, h)),
        out_shape=jax.ShapeDtypeStruct((b, s, WIDTH_A), BF16),
        scratch_shapes=[pltpu.VMEM((3, DEINT, s // DEINT, HEAD_DIM), F32)]
        + [pltpu.VMEM((DEINT, s // DEINT, HEAD_DIM), F32)] * 3
        + [pltpu.VMEM((s, HEAD_DIM), F32), pltpu.VMEM((4, HEAD_DIM, 2 * HEAD_DIM), F32)],
        compiler_params=_cparams("parallel", "parallel"),
        name="dilated_attn",
    )(view, view, view)
    return o.reshape(b * s, WIDTH_A)


def _sb_kernel(q_ref, k_ref, v_ref, o_ref, *, blk, nq, hg):
    row = lax.broadcasted_iota(jnp.int32, (blk, blk), 0)
    col = lax.broadcasted_iota(jnp.int32, (blk, blk), 1)
    tri = jnp.where(row > col, 1.0, 0.0).astype(BF16)
    causal = col < row
    heads = [slice(h * HEAD_DIM, (h + 1) * HEAD_DIM) for h in range(hg)]

    def qblock(i, carry):
        q0 = pl.multiple_of(i * blk, blk)
        qs = [q_ref[pl.ds(q0, blk), cs] for cs in heads]

        def step(st, diagonal):
            kb, cs_, accs = st
            k0 = pl.multiple_of(kb * blk, blk)
            logits, parts, new_c, new_acc = [], [], [], []
            for q, c, cs in zip(qs, cs_, heads):
                z = _dot_nt(q, k_ref[pl.ds(k0, blk), cs])
                sp = jnp.maximum(z, 0.0) + jnp.log2(1.0 + jnp.exp2(-jnp.abs(z)))
                lm = jnp.where(causal, -sp, 0.0) if diagonal else -sp
                hi = lm.astype(BF16)
                parts += [hi, (lm - hi.astype(F32)).astype(BF16)]
                logits.append(z - sp + c)
                new_c.append(c + jnp.sum(lm, axis=-1, keepdims=True))
            later = _dot(jnp.concatenate(parts, axis=0), tri)
            for h, (lg, acc, cs) in enumerate(zip(logits, accs, heads)):
                a = jnp.exp2(lg + later[2 * h * blk:(2 * h + 1) * blk] + later[(2 * h + 1) * blk:(2 * h + 2) * blk])
                if diagonal:
                    a = jnp.where(causal, a, 0.0)
                new_acc.append(acc + _dot(a.astype(BF16), v_ref[pl.ds(k0, blk), cs]))
            return kb - 1, tuple(new_c), tuple(new_acc)

        def cond(st):
            kb, cs_, _ = st
            live = jnp.max(functools.reduce(jnp.maximum, cs_)) > F32_EXP_ZERO * LOG2E
            return jnp.logical_and(kb >= 0, live)

        init = (i, tuple(jnp.zeros((blk, 1), F32) for _ in heads),
                tuple(jnp.zeros((blk, HEAD_DIM), F32) for _ in heads))
        _, _, accs = lax.while_loop(cond, lambda st: step(st, False), step(init, True))
        for acc, cs in zip(accs, heads):
            o_ref[pl.ds(q0, blk), cs] = acc.astype(BF16)
        return carry

    lax.fori_loop(0, nq, qblock, 0)


def _sb_attn(pb, b, s):
    view = pb.reshape(b, s, pb.shape[1])
    blk = min(s, 256)
    hg = 4
    cw = hg * HEAD_DIM
    ngrp = N_HEADS_B // hg

    def in_map(x):
        return lambda bi, g: (bi, 0, x * ngrp + g)

    o = pl.pallas_call(
        functools.partial(_sb_kernel, blk=blk, nq=s // blk, hg=hg),
        grid=(b, ngrp),
        in_specs=[pl.BlockSpec((None, s, cw), in_map(0)),
                  pl.BlockSpec((None, s, cw), in_map(1)),
                  pl.BlockSpec((None, s, cw), in_map(2))],
        out_specs=pl.BlockSpec((None, s, cw), lambda bi, g: (bi, 0, g)),
        out_shape=jax.ShapeDtypeStruct((b, s, WIDTH_B), BF16),
        compiler_params=_cparams("parallel", "parallel"),
        name="stick_breaking",
    )(view, view, view)
    return o.reshape(b * s, WIDTH_B)


def _mix_out_kernel(oa_ref, ob_ref, ga_ref, gb_ref, w_ref, h_ref, out_ref):
    na = _rms(oa_ref[...].astype(F32), ga_ref[...]).astype(BF16)
    nb = _rms(ob_ref[...].astype(F32), gb_ref[...]).astype(BF16)
    y = _dot(na, w_ref[:WIDTH_A, :]) + _dot(nb, w_ref[WIDTH_A:, :])
    out_ref[...] = h_ref[...] + y


def _mix_out(oa, ob, ga, gb, w_bf, h2):
    t, d = h2.shape
    tm = min(t, 512)
    ra = pl.BlockSpec((tm, WIDTH_A), lambda i: (i, 0))
    rd = pl.BlockSpec((tm, d), lambda i: (i, 0))
    return pl.pallas_call(
        _mix_out_kernel,
        grid=(t // tm,),
        in_specs=[ra, ra,
                  pl.BlockSpec((1, WIDTH_A), lambda i: (0, 0)),
                  pl.BlockSpec((1, WIDTH_B), lambda i: (0, 0)),
                  pl.BlockSpec(w_bf.shape, lambda i: (0, 0)),
                  rd],
        out_specs=rd,
        out_shape=jax.ShapeDtypeStruct((t, d), F32),
        compiler_params=_cparams("parallel"),
        name="mix_out_proj",
    )(oa, ob, ga.reshape(1, -1), gb.reshape(1, -1), w_bf, h2)


def _peer_scores_kernel(h_ref, g_ref, wq_ref, sub_ref, xt_ref, scnh_ref, s2_ref):
    nh, nk = PEER_HEADS, PEER_N_KEYS
    rc = 2 * LANES
    for r0 in range(0, h_ref.shape[0], rc):
        xn = _rms(h_ref[r0:r0 + rc, :], g_ref[...])
        xt_ref[:, r0:r0 + rc] = xn.T.astype(BF16)
        qr = _dot(xn.astype(BF16), wq_ref[...]).astype(BF16)
        for h in range(nh):
            for c in range(2):
                col = (h * 2 + c) * nk
                sc = _dot_nt(sub_ref[h, c], qr[:, col:col + nk])
                for l in range(rc // LANES):
                    lt = r0 // LANES + l
                    sl = sc[:, l * LANES:(l + 1) * LANES]
                    scnh_ref[lt, pl.ds(c * nk * nh + h, nk, stride=nh), :] = sl
                    if c == 1:
                        s2_ref[lt, h * nk:(h + 1) * nk, :] = sl


def _peer_scores(h2, g, wq_bf, sub_bf):
    t, d = h2.shape
    nh, _, nk, _ = sub_bf.shape
    tm = min(t, 512)
    return pl.pallas_call(
        _peer_scores_kernel,
        grid=(t // tm,),
        in_specs=[pl.BlockSpec((tm, d), lambda i: (i, 0)),
                  pl.BlockSpec((1, d), lambda i: (0, 0)),
                  pl.BlockSpec(wq_bf.shape, lambda i: (0, 0)),
                  pl.BlockSpec(sub_bf.shape, lambda i: (0, 0, 0, 0))],
        out_specs=[pl.BlockSpec((d, tm), lambda i: (0, i)),
                   pl.BlockSpec((tm // LANES, 2 * nk * nh, LANES), lambda i: (i, 0, 0)),
                   pl.BlockSpec((tm // LANES, nh * nk, LANES), lambda i: (i, 0, 0))],
        out_shape=[jax.ShapeDtypeStruct((d, t), BF16),
                   jax.ShapeDtypeStruct((t // LANES, 2 * nk * nh, LANES), F32),
                   jax.ShapeDtypeStruct((t // LANES, nh * nk, LANES), F32)],
        compiler_params=_cparams("parallel"),
        name="peer_scores",
    )(h2, g.reshape(1, d), wq_bf, sub_bf)


def _oddeven_merge_sort_pairs(n):
    pairs = []

    def merge(lo, m, r):
        step = r * 2
        if step < m:
            merge(lo, m, step)
            merge(lo + r, m, step)
            for i in range(lo + r, lo + m - r, step):
                pairs.append((i, i + r))
        else:
            pairs.append((lo, lo + r))

    def sort(lo, m):
        if m > 1:
            h = m // 2
            sort(lo, h)
            sort(lo + h, h)
            merge(lo, m, 1)

    sort(0, n)
    return pairs


_SORT16 = _oddeven_merge_sort_pairs(PEER_TOPK)


def _sort_desc(vals):
    vals = list(vals)
    for i, j in _SORT16:
        a, b = vals[i], vals[j]
        vals[i], vals[j] = jnp.maximum(a, b), jnp.minimum(a, b)
    return vals


def _merge_top(a, b):
    n = len(a)
    c = [jnp.maximum(a[i], b[n - 1 - i]) for i in range(n)]
    stride = n // 2
    while stride >= 1:
        for i in range(n):
            if (i // stride) % 2 == 0:
                x, y = c[i], c[i + stride]
                c[i], c[i + stride] = jnp.maximum(x, y), jnp.minimum(x, y)
        stride //= 2
    return c


def _top_sorted(vals):
    k = PEER_TOPK
    groups = [_sort_desc(vals[i:i + k]) for i in range(0, len(vals), k)]
    while len(groups) > 1:
        groups = [_merge_top(groups[i], groups[i + 1]) for i in range(0, len(groups), 2)]
    return groups[0]


def _peer_select_kernel(scnh_ref, s2_ref, c1_ref, g1_ref, e2_ref, *, nchunk):
    nk, nh, k = PEER_N_KEYS, PEER_HEADS, PEER_TOPK
    big = -NEG_INF
    for lc in range(nchunk):
        x1 = [scnh_ref[lc, n * nh:(n + 1) * nh, :] for n in range(nk)]
        x2 = [scnh_ref[lc, (nk + n) * nh:(nk + n + 1) * nh, :] for n in range(nk)]
        a = _top_sorted(x1)
        b = _top_sorted(x2)
        cand = [a[i] + b[j] for i in range(k) for j in range(k) if (i + 1) * (j + 1) <= k]
        pad = [jnp.full_like(a[0], NEG_INF)] * (-len(cand) % k)
        top = _top_sorted(cand + pad)
        c16 = top[k - 1]
        c17 = functools.reduce(jnp.maximum, [jnp.where(c < c16, c, NEG_INF) for c in cand])
        thr = 0.5 * (c16 + c17)
        mx = a[0] + b[0]
        z = functools.reduce(lambda u, v: u + v, [jnp.where(c > thr, jnp.exp(c - mx), 0.0) for c in cand])
        inv_z = 1.0 / z
        for n in range(nk):
            keep = x1[n] >= a[k - 1]
            c1_ref[lc, n * nh:(n + 1) * nh, :] = jnp.where(keep, jnp.exp(thr - x1[n] - b[0]) * inv_z, big)
            g1_ref[lc, n * nh:(n + 1) * nh, :] = jnp.exp(x1[n] - a[0])
        for h in range(nh):
            s2 = s2_ref[lc, h * nk:(h + 1) * nk, :]
            e = jnp.exp(s2 - b[0][h:h + 1, :]) * inv_z[h:h + 1, :]
            e2_ref[lc, h * nk:(h + 1) * nk, :] = jnp.where(s2 >= b[k - 1][h:h + 1, :], e, 0.0)


def _peer_select(scnh, s2):
    nt, rows, _ = s2.shape
    nchunk = min(nt, 2)
    spec = pl.BlockSpec((nchunk, rows, LANES), lambda i: (i, 0, 0))
    return pl.pallas_call(
        functools.partial(_peer_select_kernel, nchunk=nchunk),
        grid=(nt // nchunk,),
        in_specs=[pl.BlockSpec((nchunk, scnh.shape[1], LANES), lambda i: (i, 0, 0)), spec],
        out_specs=[spec, spec, spec],
        out_shape=[jax.ShapeDtypeStruct(s2.shape, F32)] * 3,
        compiler_params=_cparams("parallel"),
        name="peer_select",
    )(scnh, s2)


def _gelu(x):
    return 0.5 * x * (1.0 + lax.erf(x * (2.0 ** -0.5)))


def _peer_mix_kernel(xt_ref, u_ref, vt_ref, e2_ref, c1_ref, g1_ref, out_ref, hid_ref, a_ref, *, te, tm, cw):
    nk, nh = PEER_N_KEYS, PEER_HEADS
    nslab = te // nk
    jrows = nk // 2
    tiles = cw // LANES

    @pl.when(pl.program_id(1) == 0)
    def _():
        out_ref[...] = jnp.zeros_like(out_ref)
        hid_ref[...] = jnp.zeros_like(hid_ref)

    for c in range(tm // cw):
        for l in range(tiles):
            lt = c * tiles + l
            for jh in range(2):
                w = [jnp.zeros((jrows, LANES), F32) for _ in range(nslab)]
                for h in range(nh):
                    r0 = h * nk + jh * jrows
                    e2 = e2_ref[lt, r0:r0 + jrows, :]
                    for ii in range(nslab):
                        r = ii * nh + h
                        sel = e2 >= c1_ref[lt, r:r + 1, :]
                        w[ii] = w[ii] + jnp.where(sel, e2 * g1_ref[lt, r:r + 1, :], 0.0)
                for ii in range(nslab):
                    r0 = ii * nk + jh * jrows
                    a = _gelu(hid_ref[lt, r0:r0 + jrows, :]) * w[ii]
                    a_ref[r0:r0 + jrows, lt * LANES:(lt + 1) * LANES] = a.astype(BF16)
        out_ref[:, c * cw:(c + 1) * cw] += _dot(vt_ref[...], a_ref[:, c * cw:(c + 1) * cw])
        hid = _dot(u_ref[...], xt_ref[:, c * cw:(c + 1) * cw])
        for l in range(tiles):
            hid_ref[c * tiles + l] = hid[:, l * LANES:(l + 1) * LANES]


def _peer_mix(xt, u_bf, vt_bf, e2, c1, g1):
    d, t = xt.shape
    n_exp = u_bf.shape[0]
    nt, rows, _ = e2.shape
    tm = min(t, 1024)
    te = 512
    cw = min(tm, 256)
    n = n_exp // te
    crow = te // PEER_N_KEYS * PEER_HEADS
    once = pl.Buffered(1)
    tok = pl.BlockSpec((tm // LANES, rows, LANES), lambda i, s: (i, 0, 0), pipeline_mode=once)
    sel = pl.BlockSpec((tm // LANES, crow, LANES), lambda i, s: (i, jnp.maximum(s - 1, 0), 0))
    return pl.pallas_call(
        functools.partial(_peer_mix_kernel, te=te, tm=tm, cw=cw),
        grid=(t // tm, n + 1),
        in_specs=[pl.BlockSpec((d, tm), lambda i, s: (0, i), pipeline_mode=once),
                  pl.BlockSpec((te, d), lambda i, s: (jnp.minimum(s, n - 1), 0)),
                  pl.BlockSpec((d, te), lambda i, s: (0, jnp.maximum(s - 1, 0))),
                  tok, sel, sel],
        out_specs=pl.BlockSpec((d, tm), lambda i, s: (0, i)),
        out_shape=jax.ShapeDtypeStruct((d, t), F32),
        scratch_shapes=[pltpu.VMEM((tm // LANES, te, LANES), F32), pltpu.VMEM((te, tm), BF16)],
        compiler_params=_cparams("parallel", "arbitrary"),
        name="peer_mix",
    )(xt, u_bf, vt_bf, e2, c1, g1)


def _ple_kernel(h_ref, pt_ref, g_ref, wg_ref, p_ref, wp_ref, fn_ref, out_ref, *, final):
    rc = 256
    for r0 in range(0, h_ref.shape[0], rc):
        h = h_ref[r0:r0 + rc, :] + pt_ref[:, r0:r0 + rc].T
        hn = _rms(h, g_ref[...]).astype(BF16)
        gate = jax.nn.sigmoid(_dot(hn, wg_ref[...]))
        emb = _dot(p_ref[r0:r0 + rc, :].astype(BF16), wp_ref[...])
        y = h + emb * gate
        if final:
            y = _rms(y, fn_ref[...])
        out_ref[r0:r0 + rc, :] = y


def _ple(h2, peer_t, g, wg_bf, p2, wp_bf, fn, final):
    t, d = h2.shape
    tm = min(t, 512)
    rd = pl.BlockSpec((tm, d), lambda i: (i, 0))
    one = pl.BlockSpec((1, d), lambda i: (0, 0))
    full = lambda a: pl.BlockSpec(a.shape, lambda i: (0, 0))
    return pl.pallas_call(
        functools.partial(_ple_kernel, final=final),
        grid=(t // tm,),
        in_specs=[rd, pl.BlockSpec((d, tm), lambda i: (0, i)), one, full(wg_bf),
                  pl.BlockSpec((tm, p2.shape[1]), lambda i: (i, 0)), full(wp_bf), one],
        out_specs=rd,
        out_shape=jax.ShapeDtypeStruct((t, d), F32),
        compiler_params=_cparams("parallel"),
        name="ple_final" if final else "ple",
    )(h2, peer_t, g.reshape(1, d), wg_bf, p2, wp_bf, fn.reshape(1, d))


def kernel(x, p, positions, attn_norm, w_in, out_norm_a, out_norm_b, w_out, ffn_norm, peer_wq, peer_subkeys,
           peer_u, peer_v, ple_norm, ple_gate, ple_proj, final_norm):
    b, s, d = x.shape
    depth = w_in.shape[0]
    t = b * s
    assert s % (LANES * max(dil for _, dil in DILATED_PATTERNS)) == 0
    assert all(win // dil == HEAD_DIM for win, dil in DILATED_PATTERNS)
    cos, sin = _rope_tables(positions)
    h = x.reshape(t, d)
    for i in range(depth):
        pa, pb = _in_proj(h, attn_norm[i], w_in[i].astype(BF16), cos, sin)
        oa = _dilated_attn(pa, b, s)
        ob = _sb_attn(pb, b, s)
        h = _mix_out(oa, ob, out_norm_a[i], out_norm_b[i], w_out[i].astype(BF16), h)
        xt, scnh, s2 = _peer_scores(h, ffn_norm[i], peer_wq[i].astype(BF16), peer_subkeys[i].astype(BF16))
        c1, g1, e2 = _peer_select(scnh, s2)
        peer_t = _peer_mix(xt, peer_u[i].astype(BF16), peer_v[i].T.astype(BF16), e2, c1, g1)
        h = _ple(h, peer_t, ple_norm[i], ple_gate[i].astype(BF16), p[i].reshape(t, -1),
                 ple_proj[i].astype(BF16), final_norm, final=(i == depth - 1))
    return h.reshape(b, s, d)
```

```python
import functools
import math

import jax
import jax.numpy as jnp
from jax import lax
from jax.experimental import pallas as pl
from jax.experimental.pallas import tpu as pltpu

F32 = jnp.float32
BF16 = jnp.bfloat16

HEAD_DIM = 128
N_HEADS_A = 8
N_HEADS_B = 8
WIDTH_A = N_HEADS_A * HEAD_DIM
WIDTH_B = N_HEADS_B * HEAD_DIM
DILATED_PATTERNS = ((128, 1), (512, 4), (2048, 16))
ROPE_THETA = 10000.0
PEER_HEADS = 8
PEER_N_KEYS = 128
PEER_TOPK = 16
NORM_EPS = 1e-6
NEG_INF = -1e30

LANES = 128
VMEM_LIMIT = 56 * 1024 * 1024
TOKEN_TILE = 1024
DENSE_TILE = 512
ROW_CHUNK = 256
SB_BLOCK = 256
SB_HEADS = 4
PEER_EXPERT_TILE = 512
PEER_LANE_CHUNK = 256
BLOCKS_IN_FLIGHT = 8
F32_EXP_ZERO = -104.0
LOG2E = math.log2(math.e)


def _cparams(*sem):
    return pltpu.CompilerParams(dimension_semantics=sem, vmem_limit_bytes=VMEM_LIMIT)


def _dot(a, b):
    return jnp.dot(a, b, preferred_element_type=F32)


def _dot_nt(a, b):
    return lax.dot_general(a, b, (((1,), (1,)), ((), ())), preferred_element_type=F32)


def _rms(x, g):
    return x * lax.rsqrt(jnp.mean(x * x, axis=-1, keepdims=True) + NORM_EPS) * g


def _rope_kernel(pos_ref, freq_ref, sign_ref, cos_ref, sin_ref):
    ang = pos_ref[...].astype(F32) * freq_ref[...]
    cos_ref[...] = jnp.cos(ang)
    sin_ref[...] = jnp.sin(ang) * sign_ref[...]


def _rope_tables(positions):
    t = positions.size
    half = HEAD_DIM // 2
    inv_freq = ROPE_THETA ** (-jnp.arange(half, dtype=F32) / half)
    freq = jnp.concatenate([inv_freq, inv_freq]).reshape(1, HEAD_DIM)
    sign = jnp.concatenate([-jnp.ones((half,), F32), jnp.ones((half,), F32)]).reshape(1, HEAD_DIM)
    tm = min(t, TOKEN_TILE)
    row = pl.BlockSpec((tm, HEAD_DIM), lambda i: (i, 0))
    one = pl.BlockSpec((1, HEAD_DIM), lambda i: (0, 0))
    return pl.pallas_call(
        _rope_kernel,
        grid=(t // tm,),
        in_specs=[pl.BlockSpec((tm, 1), lambda i: (i, 0)), one, one],
        out_specs=[row, row],
        out_shape=[jax.ShapeDtypeStruct((t, HEAD_DIM), F32)] * 2,
        compiler_params=_cparams("parallel"),
        name="rope_tables",
    )(positions.reshape(t, 1), freq, sign)


def _in_proj_kernel(x_ref, g_ref, w_ref, cos_ref, sin_ref, pa_ref, pb_ref, xn_ref, *, scale):
    j = pl.program_id(1)

    @pl.when(j == 0)
    def _():
        xn_ref[...] = _rms(x_ref[...], g_ref[...]).astype(BF16)

    sca = jnp.where(j == 0, scale, 1.0).astype(F32)
    c = jnp.where(j < 2, cos_ref[...] * sca, 1.0)
    s = jnp.where(j < 2, sin_ref[...] * sca, 0.0)
    scb = jnp.where(j == 3, scale * LOG2E, 1.0).astype(F32)
    cw = 2 * HEAD_DIM
    for n0 in range(0, w_ref.shape[1], cw):
        acc = _dot(xn_ref[...], w_ref[:, n0:n0 + cw])
        for cs in (slice(0, HEAD_DIM), slice(HEAD_DIM, cw)):
            a = acc[:, cs]
            pa_ref[:, n0 + cs.start:n0 + cs.stop] = a * c + pltpu.roll(a, HEAD_DIM // 2, 1) * s
        pb_ref[:, n0:n0 + cw] = (acc * scb).astype(BF16)


def _in_proj(h2, g, w_bf, cos, sin):
    t, d = h2.shape
    n = w_bf.shape[1]
    tm = min(t, TOKEN_TILE)
    tn = WIDTH_A
    na = 3
    return pl.pallas_call(
        functools.partial(_in_proj_kernel, scale=HEAD_DIM ** -0.5),
        grid=(t // tm, n // tn),
        in_specs=[
            pl.BlockSpec((tm, d), lambda i, j: (i, 0)),
            pl.BlockSpec((1, d), lambda i, j: (0, 0)),
            pl.BlockSpec((d, tn), lambda i, j: (0, j)),
            pl.BlockSpec((tm, HEAD_DIM), lambda i, j: (i, 0)),
            pl.BlockSpec((tm, HEAD_DIM), lambda i, j: (i, 0)),
        ],
        out_specs=[pl.BlockSpec((tm, tn), lambda i, j: (i, jnp.minimum(j, na))),
                   pl.BlockSpec((tm, tn), lambda i, j: (i, jnp.where(j < na, na, j - na)))],
        out_shape=[jax.ShapeDtypeStruct((t, (na + 1) * tn), F32),
                   jax.ShapeDtypeStruct((t, n - na * tn + tn), BF16)],
        scratch_shapes=[pltpu.VMEM((tm, d), BF16)],
        compiler_params=_cparams("parallel", "arbitrary"),
        name="in_proj",
    )(h2, g.reshape(1, d), w_bf, cos, sin)


DEINT = 4


def _dilated_kernel(q_ref, k_ref, v_ref, o_ref, x4, num, den, mac, onat, bias_ref, *, seq):
    w = HEAD_DIM
    nr = DEINT
    cq = w // nr
    qi = lax.broadcasted_iota(jnp.int32, (w, 2 * w), 0)
    kj = lax.broadcasted_iota(jnp.int32, (w, 2 * w), 1)
    sh = cq.bit_length() - 1
    qpos = (qi, nr * (qi & (cq - 1)) + (qi >> sh))
    kpos = (kj, nr * (kj & (2 * cq - 1)) + (kj >> (sh + 1)))
    for order in range(2):
        for later_block, off in ((0, 0), (1, w)):
            delta = qpos[order] - kpos[order] + (off - w // 2)
            bias_ref[2 * order + later_block] = jnp.where(jnp.abs(delta) <= w // 2, 0.0, NEG_INF)
    ones = jnp.ones((2 * w, w), BF16)

    def deinterleave(c, carry):
        src = pl.ds(pl.multiple_of(c * (nr * w), nr * w), nr * w)
        dst = pl.ds(pl.multiple_of(c * w, w), w)
        for t, ref in enumerate((q_ref, k_ref, v_ref)):
            for r in range(nr):
                x4[t, r, dst, :] = ref.at[src][pl.ds(r, w, stride=nr), :]
        return carry

    lax.fori_loop(0, seq // (nr * w), deinterleave, 0)

    def attend(q, k2, v2, bias, acc, first):
        s = _dot_nt(q.astype(BF16), k2.astype(BF16)) + bias
        m = jnp.max(s, axis=-1, keepdims=True)
        e = jnp.exp(s - m).astype(BF16)
        pv = _dot(e, jnp.concatenate([v2.astype(BF16), ones], axis=1))
        (ld_n, st_n), (ld_d, st_d), (ld_m, st_m) = acc
        if first:
            st_n(pv[:, :w])
            st_d(pv[:, w:])
            st_m(jnp.broadcast_to(m, (w, w)))
        else:
            m0 = ld_m()
            mn = jnp.maximum(m0, m)
            a0 = jnp.exp(m0 - mn)
            a1 = jnp.exp(m - mn)
            st_n(a0 * ld_n() + a1 * pv[:, :w])
            st_d(a0 * ld_d() + a1 * pv[:, w:])
            st_m(mn)

    def block_d1(b):
        pb = jnp.maximum(b - 1, 0)
        qrow = pl.ds(pl.multiple_of(b * cq, cq), cq)
        krow = pl.ds(pl.multiple_of(pb * cq, cq), 2 * cq)
        q = jnp.concatenate([x4[0, r, qrow, :] for r in range(nr)], axis=0)
        k2 = jnp.concatenate([x4[1, r, krow, :] for r in range(nr)], axis=0)
        v2 = jnp.concatenate([x4[2, r, krow, :] for r in range(nr)], axis=0)

        def chunked(ref):
            def store(val):
                for r in range(nr):
                    ref[r, qrow, :] = val[r * cq:(r + 1) * cq]
            return (lambda: jnp.concatenate([ref[r, qrow, :] for r in range(nr)], axis=0)), store

        attend(q, k2, v2, bias_ref[2 + b - pb], [chunked(num), chunked(den), chunked(mac)], True)

    def block_dn(blk, r, sub, nsub):
        span = w * nsub
        pblk = jnp.maximum(blk - 1, 0)
        qwin = pl.ds(pl.multiple_of(blk * span, span), span)
        kwin = pl.ds(pl.multiple_of(pblk * span, span), 2 * span)
        rows = (lambda n: pl.ds(sub, n, stride=nsub)) if nsub > 1 else (lambda n: pl.ds(0, n))

        def strided(ref):
            view = ref.at[r, qwin]
            def store(val):
                view[rows(w), :] = val
            return (lambda: view[rows(w), :]), store

        attend(x4.at[0, r, qwin][rows(w), :], x4.at[1, r, kwin][rows(2 * w), :], x4.at[2, r, kwin][rows(2 * w), :],
               bias_ref[blk - pblk], [strided(num), strided(den), strided(mac)], False)

    for _, d in DILATED_PATTERNS:
        nblk = seq // d // w
        if d == 1:
            ub = BLOCKS_IN_FLIGHT

            def body(g, carry, ub=ub):
                for u in range(ub):
                    block_d1(g * ub + u)
                return carry
        else:
            nsub = d // nr
            ub = max(1, BLOCKS_IN_FLIGHT // d)

            def body(g, carry, ub=ub, nsub=nsub):
                for u in range(ub):
                    for r in range(nr):
                        for sub in range(nsub):
                            block_dn(g * ub + u, r, sub, nsub)
                return carry

        lax.fori_loop(0, nblk // ub, body, 0)

    def interleave(c, carry):
        src = pl.ds(pl.multiple_of(c * w, w), w)
        dst = pl.ds(pl.multiple_of(c * (nr * w), nr * w), nr * w)
        for r in range(nr):
            onat.at[dst][pl.ds(r, w, stride=nr), :] = num[r, src, :] / den[r, src, :]
        return carry

    lax.fori_loop(0, seq // (nr * w), interleave, 0)
    o_ref[...] = onat[...].astype(BF16)


def _dilated_attn(pa, b, s):
    view = pa.reshape(b, s, pa.shape[1])

    def in_map(x):
        return lambda bi, h: (bi, 0, x * N_HEADS_A + h)

    spec = lambda x: pl.BlockSpec((None, s, HEAD_DIM), in_map(x))
    o = pl.pallas_call(
        functools.partial(_dilated_kernel, seq=s),
        grid=(b, N_HEADS_A),
        in_specs=[spec(0), spec(1), spec(2)],
        out_specs=pl.BlockSpec((None, s, HEAD_DIM), lambda bi, h: (bi, 0, h)),
        out_shape=jax.ShapeDtypeStruct((b, s, WIDTH_A), BF16),
        scratch_shapes=[pltpu.VMEM((3, DEINT, s // DEINT, HEAD_DIM), F32)]
        + [pltpu.VMEM((DEINT, s // DEINT, HEAD_DIM), F32)] * 3
        + [pltpu.VMEM((s, HEAD_DIM), F32), pltpu.VMEM((4, HEAD_DIM, 2 * HEAD_DIM), F32)],
        compiler_params=_cparams("parallel", "parallel"),
        name="dilated_attn",
    )(view, view, view)
    return o.reshape(b * s, WIDTH_A)


def _sb_kernel(q_ref, k_ref, v_ref, o_ref, *, blk, nq, hg):
    row = lax.broadcasted_iota(jnp.int32, (blk, blk), 0)
    col = lax.broadcasted_iota(jnp.int32, (blk, blk), 1)
    tri = jnp.where(row > col, 1.0, 0.0).astype(BF16)
    causal = col < row
    heads = [slice(h * HEAD_DIM, (h + 1) * HEAD_DIM) for h in range(hg)]

    def qblock(i, carry):
        q0 = pl.multiple_of(i * blk, blk)
        qs = [q_ref[pl.ds(q0, blk), cs] for cs in heads]

        def step(st, diagonal):
            kb, cs_, accs = st
            k0 = pl.multiple_of(kb * blk, blk)
            logits, parts, new_c, new_acc = [], [], [], []
            for q, c, cs in zip(qs, cs_, heads):
                z = _dot_nt(q, k_ref[pl.ds(k0, blk), cs])
                sp = jnp.maximum(z, 0.0) + jnp.log2(1.0 + jnp.exp2(-jnp.abs(z)))
                lm = jnp.where(causal, -sp, 0.0) if diagonal else -sp
                hi = lm.astype(BF16)
                parts += [hi, (lm - hi.astype(F32)).astype(BF16)]
                logits.append(z - sp + c)
                new_c.append(c + jnp.sum(lm, axis=-1, keepdims=True))
            later = _dot(jnp.concatenate(parts, axis=0), tri)
            for h, (lg, acc, cs) in enumerate(zip(logits, accs, heads)):
                a = jnp.exp2(lg + later[2 * h * blk:(2 * h + 1) * blk] + later[(2 * h + 1) * blk:(2 * h + 2) * blk])
                if diagonal:
                    a = jnp.where(causal, a, 0.0)
                new_acc.append(acc + _dot(a.astype(BF16), v_ref[pl.ds(k0, blk), cs]))
            return kb - 1, tuple(new_c), tuple(new_acc)

        def cond(st):
            kb, cs_, _ = st
            live = jnp.max(functools.reduce(jnp.maximum, cs_)) > F32_EXP_ZERO * LOG2E
            return jnp.logical_and(kb >= 0, live)

        init = (i, tuple(jnp.zeros((blk, 1), F32) for _ in heads),
                tuple(jnp.zeros((blk, HEAD_DIM), F32) for _ in heads))
        _, _, accs = lax.while_loop(cond, lambda st: step(st, False), step(init, True))
        for acc, cs in zip(accs, heads):
            o_ref[pl.ds(q0, blk), cs] = acc.astype(BF16)
        return carry

    lax.fori_loop(0, nq, qblock, 0)


def _sb_attn(pb, b, s):
    view = pb.reshape(b, s, pb.shape[1])
    blk = min(s, SB_BLOCK)
    hg = SB_HEADS
    cw = hg * HEAD_DIM
    ngrp = N_HEADS_B // hg

    def in_map(x):
        return lambda bi, g: (bi, 0, x * ngrp + g)

    o = pl.pallas_call(
        functools.partial(_sb_kernel, blk=blk, nq=s // blk, hg=hg),
        grid=(b, ngrp),
        in_specs=[pl.BlockSpec((None, s, cw), in_map(0)),
                  pl.BlockSpec((None, s, cw), in_map(1)),
                  pl.BlockSpec((None, s, cw), in_map(2))],
        out_specs=pl.BlockSpec((None, s, cw), lambda bi, g: (bi, 0, g)),
        out_shape=jax.ShapeDtypeStruct((b, s, WIDTH_B), BF16),
        compiler_params=_cparams("parallel", "parallel"),
        name="stick_breaking",
    )(view, view, view)
    return o.reshape(b * s, WIDTH_B)


def _mix_out_kernel(oa_ref, ob_ref, ga_ref, gb_ref, w_ref, h_ref, out_ref):
    na = _rms(oa_ref[...].astype(F32), ga_ref[...]).astype(BF16)
    nb = _rms(ob_ref[...].astype(F32), gb_ref[...]).astype(BF16)
    y = _dot(na, w_ref[:WIDTH_A, :]) + _dot(nb, w_ref[WIDTH_A:, :])
    out_ref[...] = h_ref[...] + y


def _mix_out(oa, ob, ga, gb, w_bf, h2):
    t, d = h2.shape
    tm = min(t, DENSE_TILE)
    ra = pl.BlockSpec((tm, WIDTH_A), lambda i: (i, 0))
    rd = pl.BlockSpec((tm, d), lambda i: (i, 0))
    return pl.pallas_call(
        _mix_out_kernel,
        grid=(t // tm,),
        in_specs=[ra, ra,
                  pl.BlockSpec((1, WIDTH_A), lambda i: (0, 0)),
                  pl.BlockSpec((1, WIDTH_B), lambda i: (0, 0)),
                  pl.BlockSpec(w_bf.shape, lambda i: (0, 0)),
                  rd],
        out_specs=rd,
        out_shape=jax.ShapeDtypeStruct((t, d), F32),
        compiler_params=_cparams("parallel"),
        name="mix_out_proj",
    )(oa, ob, ga.reshape(1, -1), gb.reshape(1, -1), w_bf, h2)


def _peer_scores_kernel(h_ref, g_ref, wq_ref, sub_ref, xt_ref, scnh_ref, s2_ref):
    nh, nk = PEER_HEADS, PEER_N_KEYS
    rc = ROW_CHUNK
    for r0 in range(0, h_ref.shape[0], rc):
        xn = _rms(h_ref[r0:r0 + rc, :], g_ref[...])
        xt_ref[:, r0:r0 + rc] = xn.T.astype(BF16)
        qr = _dot(xn.astype(BF16), wq_ref[...]).astype(BF16)
        for h in range(nh):
            for c in range(2):
                col = (h * 2 + c) * nk
                sc = _dot_nt(sub_ref[h, c], qr[:, col:col + nk])
                for l in range(rc // LANES):
                    lt = r0 // LANES + l
                    sl = sc[:, l * LANES:(l + 1) * LANES]
                    scnh_ref[lt, pl.ds(c * nk * nh + h, nk, stride=nh), :] = sl
                    if c == 1:
                        s2_ref[lt, h * nk:(h + 1) * nk, :] = sl


def _peer_scores(h2, g, wq_bf, sub_bf):
    t, d = h2.shape
    nh, _, nk, _ = sub_bf.shape
    tm = min(t, DENSE_TILE)
    return pl.pallas_call(
        _peer_scores_kernel,
        grid=(t // tm,),
        in_specs=[pl.BlockSpec((tm, d), lambda i: (i, 0)),
                  pl.BlockSpec((1, d), lambda i: (0, 0)),
                  pl.BlockSpec(wq_bf.shape, lambda i: (0, 0)),
                  pl.BlockSpec(sub_bf.shape, lambda i: (0, 0, 0, 0))],
        out_specs=[pl.BlockSpec((d, tm), lambda i: (0, i)),
                   pl.BlockSpec((tm // LANES, 2 * nk * nh, LANES), lambda i: (i, 0, 0)),
                   pl.BlockSpec((tm // LANES, nh * nk, LANES), lambda i: (i, 0, 0))],
        out_shape=[jax.ShapeDtypeStruct((d, t), BF16),
                   jax.ShapeDtypeStruct((t // LANES, 2 * nk * nh, LANES), F32),
                   jax.ShapeDtypeStruct((t // LANES, nh * nk, LANES), F32)],
        compiler_params=_cparams("parallel"),
        name="peer_scores",
    )(h2, g.reshape(1, d), wq_bf, sub_bf)


def _oddeven_merge_sort_pairs(n):
    pairs = []

    def merge(lo, m, r):
        step = r * 2
        if step < m:
            merge(lo, m, step)
            merge(lo + r, m, step)
            for i in range(lo + r, lo + m - r, step):
                pairs.append((i, i + r))
        else:
            pairs.append((lo, lo + r))

    def sort(lo, m):
        if m > 1:
            h = m // 2
            sort(lo, h)
            sort(lo + h, h)
            merge(lo, m, 1)

    sort(0, n)
    return pairs


_SORT16 = _oddeven_merge_sort_pairs(PEER_TOPK)


def _sort_desc(vals):
    vals = list(vals)
    for i, j in _SORT16:
        a, b = vals[i], vals[j]
        vals[i], vals[j] = jnp.maximum(a, b), jnp.minimum(a, b)
    return vals


def _merge_top(a, b):
    n = len(a)
    c = [jnp.maximum(a[i], b[n - 1 - i]) for i in range(n)]
    stride = n // 2
    while stride >= 1:
        for i in range(n):
            if (i // stride) % 2 == 0:
                x, y = c[i], c[i + stride]
                c[i], c[i + stride] = jnp.maximum(x, y), jnp.minimum(x, y)
        stride //= 2
    return c


def _top_sorted(vals):
    k = PEER_TOPK
    groups = [_sort_desc(vals[i:i + k]) for i in range(0, len(vals), k)]
    while len(groups) > 1:
        groups = [_merge_top(groups[i], groups[i + 1]) for i in range(0, len(groups), 2)]
    return groups[0]


def _peer_select_kernel(scnh_ref, s2_ref, c1_ref, g1_ref, e2_ref, *, nchunk):
    nk, nh, k = PEER_N_KEYS, PEER_HEADS, PEER_TOPK
    big = -NEG_INF
    for lc in range(nchunk):
        x1 = [scnh_ref[lc, n * nh:(n + 1) * nh, :] for n in range(nk)]
        x2 = [scnh_ref[lc, (nk + n) * nh:(nk + n + 1) * nh, :] for n in range(nk)]
        a = _top_sorted(x1)
        b = _top_sorted(x2)
        cand = [a[i] + b[j] for i in range(k) for j in range(k) if (i + 1) * (j + 1) <= k]
        pad = [jnp.full_like(a[0], NEG_INF)] * (-len(cand) % k)
        top = _top_sorted(cand + pad)
        c16 = top[k - 1]
        c17 = functools.reduce(jnp.maximum, [jnp.where(c < c16, c, NEG_INF) for c in cand])
        thr = 0.5 * (c16 + c17)
        mx = a[0] + b[0]
        z = functools.reduce(lambda u, v: u + v, [jnp.where(c > thr, jnp.exp(c - mx), 0.0) for c in cand])
        inv_z = 1.0 / z
        for n in range(nk):
            keep = x1[n] >= a[k - 1]
            c1_ref[lc, n * nh:(n + 1) * nh, :] = jnp.where(keep, jnp.exp(thr - x1[n] - b[0]) * inv_z, big)
            g1_ref[lc, n * nh:(n + 1) * nh, :] = jnp.exp(x1[n] - a[0])
        for h in range(nh):
            s2 = s2_ref[lc, h * nk:(h + 1) * nk, :]
            e = jnp.exp(s2 - b[0][h:h + 1, :]) * inv_z[h:h + 1, :]
            e2_ref[lc, h * nk:(h + 1) * nk, :] = jnp.where(s2 >= b[k - 1][h:h + 1, :], e, 0.0)


def _peer_select(scnh, s2):
    nt, rows, _ = s2.shape
    nchunk = min(nt, 2)
    spec = pl.BlockSpec((nchunk, rows, LANES), lambda i: (i, 0, 0))
    return pl.pallas_call(
        functools.partial(_peer_select_kernel, nchunk=nchunk),
        grid=(nt // nchunk,),
        in_specs=[pl.BlockSpec((nchunk, scnh.shape[1], LANES), lambda i: (i, 0, 0)), spec],
        out_specs=[spec, spec, spec],
        out_shape=[jax.ShapeDtypeStruct(s2.shape, F32)] * 3,
        compiler_params=_cparams("parallel"),
        name="peer_select",
    )(scnh, s2)


def _gelu(x):
    return 0.5 * x * (1.0 + lax.erf(x * (2.0 ** -0.5)))


def _peer_mix_kernel(xt_ref, u_ref, vt_ref, e2_ref, c1_ref, g1_ref, out_ref, hid_ref, a_ref, *, te, tm, cw):
    nk, nh = PEER_N_KEYS, PEER_HEADS
    nslab = te // nk
    jrows = nk // 2
    tiles = cw // LANES
    s = pl.program_id(1)
    last = pl.num_programs(1) - 1

    def project(c):
        for l in range(tiles):
            lt = c * tiles + l
            for jh in range(2):
                w = [jnp.zeros((jrows, LANES), F32) for _ in range(nslab)]
                for h in range(nh):
                    r0 = h * nk + jh * jrows
                    e2 = e2_ref[lt, r0:r0 + jrows, :]
                    for ii in range(nslab):
                        r = ii * nh + h
                        sel = e2 >= c1_ref[lt, r:r + 1, :]
                        w[ii] = w[ii] + jnp.where(sel, e2 * g1_ref[lt, r:r + 1, :], 0.0)
                for ii in range(nslab):
                    r0 = ii * nk + jh * jrows
                    a = _gelu(hid_ref[lt, r0:r0 + jrows, :]) * w[ii]
                    a_ref[r0:r0 + jrows, lt * LANES:(lt + 1) * LANES] = a.astype(BF16)
        out_ref[:, c * cw:(c + 1) * cw] += _dot(vt_ref[...], a_ref[:, c * cw:(c + 1) * cw])

    def hidden(c):
        hid = _dot(u_ref[...], xt_ref[:, c * cw:(c + 1) * cw])
        for l in range(tiles):
            hid_ref[c * tiles + l] = hid[:, l * LANES:(l + 1) * LANES]

    @pl.when(s == 0)
    def _():
        out_ref[...] = jnp.zeros_like(out_ref)
        for c in range(tm // cw):
            hidden(c)

    @pl.when(jnp.logical_and(s > 0, s < last))
    def _():
        for c in range(tm // cw):
            project(c)
            hidden(c)

    @pl.when(s == last)
    def _():
        for c in range(tm // cw):
            project(c)


def _peer_mix(xt, u_bf, vt_bf, e2, c1, g1):
    d, t = xt.shape
    n_exp = u_bf.shape[0]
    nt, rows, _ = e2.shape
    tm = min(t, TOKEN_TILE)
    te = PEER_EXPERT_TILE
    cw = min(tm, PEER_LANE_CHUNK)
    n = n_exp // te
    crow = te // PEER_N_KEYS * PEER_HEADS
    once = pl.Buffered(1)
    tok = pl.BlockSpec((tm // LANES, rows, LANES), lambda i, s: (i, 0, 0), pipeline_mode=once)
    sel = pl.BlockSpec((tm // LANES, crow, LANES), lambda i, s: (i, jnp.maximum(s - 1, 0), 0))
    return pl.pallas_call(
        functools.partial(_peer_mix_kernel, te=te, tm=tm, cw=cw),
        grid=(t // tm, n + 1),
        in_specs=[pl.BlockSpec((d, tm), lambda i, s: (0, i), pipeline_mode=once),
                  pl.BlockSpec((te, d), lambda i, s: (jnp.minimum(s, n - 1), 0)),
                  pl.BlockSpec((d, te), lambda i, s: (0, jnp.maximum(s - 1, 0))),
                  tok, sel, sel],
        out_specs=pl.BlockSpec((d, tm), lambda i, s: (0, i)),
        out_shape=jax.ShapeDtypeStruct((d, t), F32),
        scratch_shapes=[pltpu.VMEM((tm // LANES, te, LANES), F32), pltpu.VMEM((te, tm), BF16)],
        compiler_params=_cparams("parallel", "arbitrary"),
        name="peer_mix",
    )(xt, u_bf, vt_bf, e2, c1, g1)


def _ple_kernel(h_ref, pt_ref, g_ref, wg_ref, p_ref, wp_ref, fn_ref, out_ref, *, final):
    rc = ROW_CHUNK
    for r0 in range(0, h_ref.shape[0], rc):
        h = h_ref[r0:r0 + rc, :] + pt_ref[:, r0:r0 + rc].T
        hn = _rms(h, g_ref[...]).astype(BF16)
        gate = jax.nn.sigmoid(_dot(hn, wg_ref[...]))
        emb = _dot(p_ref[r0:r0 + rc, :].astype(BF16), wp_ref[...])
        y = h + emb * gate
        if final:
            y = _rms(y, fn_ref[...])
        out_ref[r0:r0 + rc, :] = y


def _ple(h2, peer_t, g, wg_bf, p2, wp_bf, fn, final):
    t, d = h2.shape
    tm = min(t, DENSE_TILE)
    rd = pl.BlockSpec((tm, d), lambda i: (i, 0))
    one = pl.BlockSpec((1, d), lambda i: (0, 0))
    full = lambda a: pl.BlockSpec(a.shape, lambda i: (0, 0))
    return pl.pallas_call(
        functools.partial(_ple_kernel, final=final),
        grid=(t // tm,),
        in_specs=[rd, pl.BlockSpec((d, tm), lambda i: (0, i)), one, full(wg_bf),
                  pl.BlockSpec((tm, p2.shape[1]), lambda i: (i, 0)), full(wp_bf), one],
        out_specs=rd,
        out_shape=jax.ShapeDtypeStruct((t, d), F32),
        compiler_params=_cparams("parallel"),
        name="ple_final" if final else "ple",
    )(h2, peer_t, g.reshape(1, d), wg_bf, p2, wp_bf, fn.reshape(1, d))


def kernel(x, p, positions, attn_norm, w_in, out_norm_a, out_norm_b, w_out, ffn_norm, peer_wq, peer_subkeys,
           peer_u, peer_v, ple_norm, ple_gate, ple_proj, final_norm):
    b, s, d = x.shape
    depth = w_in.shape[0]
    t = b * s
    assert s % (LANES * max(dil for _, dil in DILATED_PATTERNS)) == 0
    assert all(win // dil == HEAD_DIM for win, dil in DILATED_PATTERNS)
    cos, sin = _rope_tables(positions)
    h = x.reshape(t, d)
    for i in range(depth):
        pa, pb = _in_proj(h, attn_norm[i], w_in[i].astype(BF16), cos, sin)
        oa = _dilated_attn(pa, b, s)
        ob = _sb_attn(pb, b, s)
        h = _mix_out(oa, ob, out_norm_a[i], out_norm_b[i], w_out[i].astype(BF16), h)
        xt, scnh, s2 = _peer_scores(h, ffn_norm[i], peer_wq[i].astype(BF16), peer_subkeys[i].astype(BF16))
        c1, g1, e2 = _peer_select(scnh, s2)
        peer_t = _peer_mix(xt, peer_u[i].astype(BF16), peer_v[i].T.astype(BF16), e2, c1, g1)
        h = _ple(h, peer_t, ple_norm[i], ple_gate[i].astype(BF16), p[i].reshape(t, -1),
                 ple_proj[i].astype(BF16), final_norm, final=(i == depth - 1))
    return h.reshape(b, s, d)
```

```python
import functools
import math

import jax
import jax.numpy as jnp
from jax import lax
from jax.experimental import pallas as pl
from jax.experimental.pallas import tpu as pltpu

F32 = jnp.float32
BF16 = jnp.bfloat16

HEAD_DIM = 128
N_HEADS_A = 8
N_HEADS_B = 8
WIDTH_A = N_HEADS_A * HEAD_DIM
WIDTH_B = N_HEADS_B * HEAD_DIM
DILATED_PATTERNS = ((128, 1), (512, 4), (2048, 16))
ROPE_THETA = 10000.0
PEER_HEADS = 8
PEER_N_KEYS = 128
PEER_TOPK = 16
NORM_EPS = 1e-6
NEG_INF = -1e30

LANES = 128
VMEM_LIMIT = 56 * 1024 * 1024
TOKEN_TILE = 1024
DENSE_TILE = 512
ROW_CHUNK = 256
SB_BLOCK = 256
SB_HEADS = 4
PEER_EXPERT_TILE = 512
PEER_LANE_CHUNK = 256
BLOCKS_IN_FLIGHT = 16
F32_EXP_ZERO = -104.0
LOG2E = math.log2(math.e)


def _cparams(*sem):
    return pltpu.CompilerParams(dimension_semantics=sem, vmem_limit_bytes=VMEM_LIMIT)


def _dot(a, b):
    return jnp.dot(a, b, preferred_element_type=F32)


def _dot_nt(a, b):
    return lax.dot_general(a, b, (((1,), (1,)), ((), ())), preferred_element_type=F32)


def _rms(x, g):
    return x * lax.rsqrt(jnp.mean(x * x, axis=-1, keepdims=True) + NORM_EPS) * g


def _rope_kernel(pos_ref, freq_ref, sign_ref, cos_ref, sin_ref):
    ang = pos_ref[...].astype(F32) * freq_ref[...]
    cos_ref[...] = jnp.cos(ang)
    sin_ref[...] = jnp.sin(ang) * sign_ref[...]


def _rope_tables(positions):
    t = positions.size
    half = HEAD_DIM // 2
    inv_freq = ROPE_THETA ** (-jnp.arange(half, dtype=F32) / half)
    freq = jnp.concatenate([inv_freq, inv_freq]).reshape(1, HEAD_DIM)
    sign = jnp.concatenate([-jnp.ones((half,), F32), jnp.ones((half,), F32)]).reshape(1, HEAD_DIM)
    tm = min(t, TOKEN_TILE)
    row = pl.BlockSpec((tm, HEAD_DIM), lambda i: (i, 0))
    one = pl.BlockSpec((1, HEAD_DIM), lambda i: (0, 0))
    return pl.pallas_call(
        _rope_kernel,
        grid=(t // tm,),
        in_specs=[pl.BlockSpec((tm, 1), lambda i: (i, 0)), one, one],
        out_specs=[row, row],
        out_shape=[jax.ShapeDtypeStruct((t, HEAD_DIM), F32)] * 2,
        compiler_params=_cparams("parallel"),
        name="rope_tables",
    )(positions.reshape(t, 1), freq, sign)


def _in_proj_kernel(x_ref, g_ref, w_ref, cos_ref, sin_ref, pa_ref, pb_ref, xn_ref, *, scale):
    j = pl.program_id(1)

    @pl.when(j == 0)
    def _():
        xn_ref[...] = _rms(x_ref[...], g_ref[...]).astype(BF16)

    sca = jnp.where(j == 0, scale, 1.0).astype(F32)
    c = jnp.where(j < 2, cos_ref[...] * sca, 1.0)
    s = jnp.where(j < 2, sin_ref[...] * sca, 0.0)
    scb = jnp.where(j == 3, scale * LOG2E, 1.0).astype(F32)
    cw = 2 * HEAD_DIM
    for n0 in range(0, w_ref.shape[1], cw):
        acc = _dot(xn_ref[...], w_ref[:, n0:n0 + cw])
        for cs in (slice(0, HEAD_DIM), slice(HEAD_DIM, cw)):
            a = acc[:, cs]
            pa_ref[:, n0 + cs.start:n0 + cs.stop] = a * c + pltpu.roll(a, HEAD_DIM // 2, 1) * s
        pb_ref[:, n0:n0 + cw] = (acc * scb).astype(BF16)


def _in_proj(h2, g, w_bf, cos, sin):
    t, d = h2.shape
    n = w_bf.shape[1]
    tm = min(t, TOKEN_TILE)
    tn = WIDTH_A
    na = 3
    return pl.pallas_call(
        functools.partial(_in_proj_kernel, scale=HEAD_DIM ** -0.5),
        grid=(t // tm, n // tn),
        in_specs=[
            pl.BlockSpec((tm, d), lambda i, j: (i, 0)),
            pl.BlockSpec((1, d), lambda i, j: (0, 0)),
            pl.BlockSpec((d, tn), lambda i, j: (0, j)),
            pl.BlockSpec((tm, HEAD_DIM), lambda i, j: (i, 0)),
            pl.BlockSpec((tm, HEAD_DIM), lambda i, j: (i, 0)),
        ],
        out_specs=[pl.BlockSpec((tm, tn), lambda i, j: (i, jnp.minimum(j, na))),
                   pl.BlockSpec((tm, tn), lambda i, j: (i, jnp.where(j < na, na, j - na)))],
        out_shape=[jax.ShapeDtypeStruct((t, (na + 1) * tn), F32),
                   jax.ShapeDtypeStruct((t, n - na * tn + tn), BF16)],
        scratch_shapes=[pltpu.VMEM((tm, d), BF16)],
        compiler_params=_cparams("parallel", "arbitrary"),
        name="in_proj",
    )(h2, g.reshape(1, d), w_bf, cos, sin)


DEINT = 4


def _dilated_kernel(q_ref, k_ref, v_ref, o_ref, x4, num, den, mac, onat, bias_ref, *, seq):
    w = HEAD_DIM
    nr = DEINT
    cq = w // nr
    qi = lax.broadcasted_iota(jnp.int32, (w, 2 * w), 0)
    kj = lax.broadcasted_iota(jnp.int32, (w, 2 * w), 1)
    sh = cq.bit_length() - 1
    qpos = (qi, nr * (qi & (cq - 1)) + (qi >> sh))
    kpos = (kj, nr * (kj & (2 * cq - 1)) + (kj >> (sh + 1)))
    for order in range(2):
        for later_block, off in ((0, 0), (1, w)):
            delta = qpos[order] - kpos[order] + (off - w // 2)
            bias_ref[2 * order + later_block] = jnp.where(jnp.abs(delta) <= w // 2, 0.0, NEG_INF)
    ones = jnp.ones((2 * w, w), BF16)

    def deinterleave(c, carry):
        src = pl.ds(pl.multiple_of(c * (nr * w), nr * w), nr * w)
        dst = pl.ds(pl.multiple_of(c * w, w), w)
        for t, ref in enumerate((q_ref, k_ref, v_ref)):
            for r in range(nr):
                x4[t, r, dst, :] = ref.at[src][pl.ds(r, w, stride=nr), :]
        return carry

    lax.fori_loop(0, seq // (nr * w), deinterleave, 0)

    def attend(q, k2, v2, bias, acc, first):
        s = _dot_nt(q.astype(BF16), k2.astype(BF16)) + bias
        m = jnp.max(s, axis=-1, keepdims=True)
        e = jnp.exp(s - m).astype(BF16)
        pv = _dot(e, jnp.concatenate([v2.astype(BF16), ones], axis=1))
        (ld_n, st_n), (ld_d, st_d), (ld_m, st_m) = acc
        if first:
            st_n(pv[:, :w])
            st_d(pv[:, w:])
            st_m(jnp.broadcast_to(m, (w, w)))
        else:
            m0 = ld_m()
            mn = jnp.maximum(m0, m)
            a0 = jnp.exp(m0 - mn)
            a1 = jnp.exp(m - mn)
            st_n(a0 * ld_n() + a1 * pv[:, :w])
            st_d(a0 * ld_d() + a1 * pv[:, w:])
            st_m(mn)

    def block_d1(b):
        pb = jnp.maximum(b - 1, 0)
        qrow = pl.ds(pl.multiple_of(b * cq, cq), cq)
        krow = pl.ds(pl.multiple_of(pb * cq, cq), 2 * cq)
        q = jnp.concatenate([x4[0, r, qrow, :] for r in range(nr)], axis=0)
        k2 = jnp.concatenate([x4[1, r, krow, :] for r in range(nr)], axis=0)
        v2 = jnp.concatenate([x4[2, r, krow, :] for r in range(nr)], axis=0)

        def chunked(ref):
            def store(val):
                for r in range(nr):
                    ref[r, qrow, :] = val[r * cq:(r + 1) * cq]
            return (lambda: jnp.concatenate([ref[r, qrow, :] for r in range(nr)], axis=0)), store

        attend(q, k2, v2, bias_ref[2 + b - pb], [chunked(num), chunked(den), chunked(mac)], True)

    def block_dn(blk, r, sub, nsub):
        span = w * nsub
        pblk = jnp.maximum(blk - 1, 0)
        qwin = pl.ds(pl.multiple_of(blk * span, span), span)
        kwin = pl.ds(pl.multiple_of(pblk * span, span), 2 * span)
        rows = (lambda n: pl.ds(sub, n, stride=nsub)) if nsub > 1 else (lambda n: pl.ds(0, n))

        def strided(ref):
            view = ref.at[r, qwin]
            def store(val):
                view[rows(w), :] = val
            return (lambda: view[rows(w), :]), store

        attend(x4.at[0, r, qwin][rows(w), :], x4.at[1, r, kwin][rows(2 * w), :], x4.at[2, r, kwin][rows(2 * w), :],
               bias_ref[blk - pblk], [strided(num), strided(den), strided(mac)], False)

    for _, d in DILATED_PATTERNS:
        nblk = seq // d // w
        if d == 1:
            ub = BLOCKS_IN_FLIGHT

            def body(g, carry, ub=ub):
                for u in range(ub):
                    block_d1(g * ub + u)
                return carry
        else:
            nsub = d // nr
            ub = max(1, BLOCKS_IN_FLIGHT // d)

            def body(g, carry, ub=ub, nsub=nsub):
                for u in range(ub):
                    for r in range(nr):
                        for sub in range(nsub):
                            block_dn(g * ub + u, r, sub, nsub)
                return carry

        lax.fori_loop(0, nblk // ub, body, 0)

    def interleave(c, carry):
        src = pl.ds(pl.multiple_of(c * w, w), w)
        dst = pl.ds(pl.multiple_of(c * (nr * w), nr * w), nr * w)
        for r in range(nr):
            onat.at[dst][pl.ds(r, w, stride=nr), :] = num[r, src, :] / den[r, src, :]
        return carry

    lax.fori_loop(0, seq // (nr * w), interleave, 0)
    o_ref[...] = onat[...].astype(BF16)


def _dilated_attn(pa, b, s):
    view = pa.reshape(b, s, pa.shape[1])

    def in_map(x):
        return lambda bi, h: (bi, 0, x * N_HEADS_A + h)

    spec = lambda x: pl.BlockSpec((None, s, HEAD_DIM), in_map(x))
    o = pl.pallas_call(
        functools.partial(_dilated_kernel, seq=s),
        grid=(b, N_HEADS_A),
        in_specs=[spec(0), spec(1), spec(2)],
        out_specs=pl.BlockSpec((None, s, HEAD_DIM), lambda bi, h: (bi, 0, h)),
        out_shape=jax.ShapeDtypeStruct((b, s, WIDTH_A), BF16),
        scratch_shapes=[pltpu.VMEM((3, DEINT, s // DEINT, HEAD_DIM), F32)]
        + [pltpu.VMEM((DEINT, s // DEINT, HEAD_DIM), F32)] * 3
        + [pltpu.VMEM((s, HEAD_DIM), F32), pltpu.VMEM((4, HEAD_DIM, 2 * HEAD_DIM), F32)],
        compiler_params=_cparams("parallel", "parallel"),
        name="dilated_attn",
    )(view, view, view)
    return o.reshape(b * s, WIDTH_A)


def _sb_kernel(q_ref, k_ref, v_ref, o_ref, *, blk, nq, hg):
    row = lax.broadcasted_iota(jnp.int32, (blk, blk), 0)
    col = lax.broadcasted_iota(jnp.int32, (blk, blk), 1)
    tri = jnp.where(row > col, 1.0, 0.0).astype(BF16)
    causal = col < row
    heads = [slice(h * HEAD_DIM, (h + 1) * HEAD_DIM) for h in range(hg)]

    def qblock(i, carry):
        q0 = pl.multiple_of(i * blk, blk)
        qs = [q_ref[pl.ds(q0, blk), cs] for cs in heads]

        def step(st, diagonal):
            kb, cs_, accs = st
            k0 = pl.multiple_of(kb * blk, blk)
            logits, parts, new_c, new_acc = [], [], [], []
            for q, c, cs in zip(qs, cs_, heads):
                z = _dot_nt(q, k_ref[pl.ds(k0, blk), cs])
                sp = jnp.maximum(z, 0.0) + jnp.log2(1.0 + jnp.exp2(-jnp.abs(z)))
                lm = jnp.where(causal, -sp, 0.0) if diagonal else -sp
                hi = lm.astype(BF16)
                parts += [hi, (lm - hi.astype(F32)).astype(BF16)]
                logits.append(z - sp + c)
                new_c.append(c + jnp.sum(lm, axis=-1, keepdims=True))
            later = _dot(jnp.concatenate(parts, axis=0), tri)
            for h, (lg, acc, cs) in enumerate(zip(logits, accs, heads)):
                a = jnp.exp2(lg + later[2 * h * blk:(2 * h + 1) * blk] + later[(2 * h + 1) * blk:(2 * h + 2) * blk])
                if diagonal:
                    a = jnp.where(causal, a, 0.0)
                new_acc.append(acc + _dot(a.astype(BF16), v_ref[pl.ds(k0, blk), cs]))
            return kb - 1, tuple(new_c), tuple(new_acc)

        def cond(st):
            kb, cs_, _ = st
            live = jnp.max(functools.reduce(jnp.maximum, cs_)) > F32_EXP_ZERO * LOG2E
            return jnp.logical_and(kb >= 0, live)

        init = (i, tuple(jnp.zeros((blk, 1), F32) for _ in heads),
                tuple(jnp.zeros((blk, HEAD_DIM), F32) for _ in heads))
        _, _, accs = lax.while_loop(cond, lambda st: step(st, False), step(init, True))
        for acc, cs in zip(accs, heads):
            o_ref[pl.ds(q0, blk), cs] = acc.astype(BF16)
        return carry

    lax.fori_loop(0, nq, qblock, 0)


def _sb_attn(pb, b, s):
    view = pb.reshape(b, s, pb.shape[1])
    blk = min(s, SB_BLOCK)
    hg = SB_HEADS
    cw = hg * HEAD_DIM
    ngrp = N_HEADS_B // hg

    def in_map(x):
        return lambda bi, g: (bi, 0, x * ngrp + g)

    o = pl.pallas_call(
        functools.partial(_sb_kernel, blk=blk, nq=s // blk, hg=hg),
        grid=(b, ngrp),
        in_specs=[pl.BlockSpec((None, s, cw), in_map(0)),
                  pl.BlockSpec((None, s, cw), in_map(1)),
                  pl.BlockSpec((None, s, cw), in_map(2))],
        out_specs=pl.BlockSpec((None, s, cw), lambda bi, g: (bi, 0, g)),
        out_shape=jax.ShapeDtypeStruct((b, s, WIDTH_B), BF16),
        compiler_params=_cparams("parallel", "parallel"),
        name="stick_breaking",
    )(view, view, view)
    return o.reshape(b * s, WIDTH_B)


def _mix_out_kernel(oa_ref, ob_ref, ga_ref, gb_ref, w_ref, h_ref, out_ref):
    rc = ROW_CHUNK
    for r0 in range(0, h_ref.shape[0], rc):
        na = _rms(oa_ref[r0:r0 + rc, :].astype(F32), ga_ref[...]).astype(BF16)
        nb = _rms(ob_ref[r0:r0 + rc, :].astype(F32), gb_ref[...]).astype(BF16)
        y = _dot(na, w_ref[:WIDTH_A, :]) + _dot(nb, w_ref[WIDTH_A:, :])
        out_ref[r0:r0 + rc, :] = h_ref[r0:r0 + rc, :] + y


def _mix_out(oa, ob, ga, gb, w_bf, h2):
    t, d = h2.shape
    tm = min(t, DENSE_TILE)
    ra = pl.BlockSpec((tm, WIDTH_A), lambda i: (i, 0))
    rd = pl.BlockSpec((tm, d), lambda i: (i, 0))
    return pl.pallas_call(
        _mix_out_kernel,
        grid=(t // tm,),
        in_specs=[ra, ra,
                  pl.BlockSpec((1, WIDTH_A), lambda i: (0, 0)),
                  pl.BlockSpec((1, WIDTH_B), lambda i: (0, 0)),
                  pl.BlockSpec(w_bf.shape, lambda i: (0, 0)),
                  rd],
        out_specs=rd,
        out_shape=jax.ShapeDtypeStruct((t, d), F32),
        compiler_params=_cparams("parallel"),
        name="mix_out_proj",
    )(oa, ob, ga.reshape(1, -1), gb.reshape(1, -1), w_bf, h2)


def _peer_scores_kernel(h_ref, g_ref, wq_ref, sub_ref, xt_ref, scnh_ref, s2_ref):
    nh, nk = PEER_HEADS, PEER_N_KEYS
    rc = ROW_CHUNK
    for r0 in range(0, h_ref.shape[0], rc):
        xn = _rms(h_ref[r0:r0 + rc, :], g_ref[...])
        xt_ref[:, r0:r0 + rc] = xn.T.astype(BF16)
        qr = _dot(xn.astype(BF16), wq_ref[...]).astype(BF16)
        for h in range(nh):
            for c in range(2):
                col = (h * 2 + c) * nk
                sc = _dot_nt(sub_ref[h, c], qr[:, col:col + nk])
                for l in range(rc // LANES):
                    lt = r0 // LANES + l
                    sl = sc[:, l * LANES:(l + 1) * LANES]
                    scnh_ref[lt, pl.ds(c * nk * nh + h, nk, stride=nh), :] = sl
                    if c == 1:
                        s2_ref[lt, h * nk:(h + 1) * nk, :] = sl


def _peer_scores(h2, g, wq_bf, sub_bf):
    t, d = h2.shape
    nh, _, nk, _ = sub_bf.shape
    tm = min(t, DENSE_TILE)
    return pl.pallas_call(
        _peer_scores_kernel,
        grid=(t // tm,),
        in_specs=[pl.BlockSpec((tm, d), lambda i: (i, 0)),
                  pl.BlockSpec((1, d), lambda i: (0, 0)),
                  pl.BlockSpec(wq_bf.shape, lambda i: (0, 0)),
                  pl.BlockSpec(sub_bf.shape, lambda i: (0, 0, 0, 0))],
        out_specs=[pl.BlockSpec((d, tm), lambda i: (0, i)),
                   pl.BlockSpec((tm // LANES, 2 * nk * nh, LANES), lambda i: (i, 0, 0)),
                   pl.BlockSpec((tm // LANES, nh * nk, LANES), lambda i: (i, 0, 0))],
        out_shape=[jax.ShapeDtypeStruct((d, t), BF16),
                   jax.ShapeDtypeStruct((t // LANES, 2 * nk * nh, LANES), F32),
                   jax.ShapeDtypeStruct((t // LANES, nh * nk, LANES), F32)],
        compiler_params=_cparams("parallel"),
        name="peer_scores",
    )(h2, g.reshape(1, d), wq_bf, sub_bf)


def _oddeven_merge_sort_pairs(n):
    pairs = []

    def merge(lo, m, r):
        step = r * 2
        if step < m:
            merge(lo, m, step)
            merge(lo + r, m, step)
            for i in range(lo + r, lo + m - r, step):
                pairs.append((i, i + r))
        else:
            pairs.append((lo, lo + r))

    def sort(lo, m):
        if m > 1:
            h = m // 2
            sort(lo, h)
            sort(lo + h, h)
            merge(lo, m, 1)

    sort(0, n)
    return pairs


_SORT16 = _oddeven_merge_sort_pairs(PEER_TOPK)


def _sort_desc(vals):
    vals = list(vals)
    for i, j in _SORT16:
        a, b = vals[i], vals[j]
        vals[i], vals[j] = jnp.maximum(a, b), jnp.minimum(a, b)
    return vals


def _merge_top(a, b):
    n = len(a)
    c = [jnp.maximum(a[i], b[n - 1 - i]) for i in range(n)]
    stride = n // 2
    while stride >= 1:
        for i in range(n):
            if (i // stride) % 2 == 0:
                x, y = c[i], c[i + stride]
                c[i], c[i + stride] = jnp.maximum(x, y), jnp.minimum(x, y)
        stride //= 2
    return c


def _top_sorted(vals):
    k = PEER_TOPK
    groups = [_sort_desc(vals[i:i + k]) for i in range(0, len(vals), k)]
    while len(groups) > 1:
        groups = [_merge_top(groups[i], groups[i + 1]) for i in range(0, len(groups), 2)]
    return groups[0]


def _peer_select_kernel(scnh_ref, s2_ref, c1_ref, g1_ref, e2_ref, *, nchunk):
    nk, nh, k = PEER_N_KEYS, PEER_HEADS, PEER_TOPK
    big = -NEG_INF
    for lc in range(nchunk):
        x1 = [scnh_ref[lc, n * nh:(n + 1) * nh, :] for n in range(nk)]
        x2 = [scnh_ref[lc, (nk + n) * nh:(nk + n + 1) * nh, :] for n in range(nk)]
        a = _top_sorted(x1)
        b = _top_sorted(x2)
        cand = [a[i] + b[j] for i in range(k) for j in range(k) if (i + 1) * (j + 1) <= k]
        pad = [jnp.full_like(a[0], NEG_INF)] * (-len(cand) % k)
        top = _top_sorted(cand + pad)
        c16 = top[k - 1]
        c17 = functools.reduce(jnp.maximum, [jnp.where(c < c16, c, NEG_INF) for c in cand])
        thr = 0.5 * (c16 + c17)
        mx = a[0] + b[0]
        z = functools.reduce(lambda u, v: u + v, [jnp.where(c > thr, jnp.exp(c - mx), 0.0) for c in cand])
        inv_z = 1.0 / z
        for n in range(nk):
            keep = x1[n] >= a[k - 1]
            c1_ref[lc, n * nh:(n + 1) * nh, :] = jnp.where(keep, jnp.exp(thr - x1[n] - b[0]) * inv_z, big)
            g1_ref[lc, n * nh:(n + 1) * nh, :] = jnp.exp(x1[n] - a[0])
        for h in range(nh):
            s2 = s2_ref[lc, h * nk:(h + 1) * nk, :]
            e = jnp.exp(s2 - b[0][h:h + 1, :]) * inv_z[h:h + 1, :]
            e2_ref[lc, h * nk:(h + 1) * nk, :] = jnp.where(s2 >= b[k - 1][h:h + 1, :], e, 0.0)


def _peer_select(scnh, s2):
    nt, rows, _ = s2.shape
    nchunk = min(nt, 2)
    spec = pl.BlockSpec((nchunk, rows, LANES), lambda i: (i, 0, 0))
    return pl.pallas_call(
        functools.partial(_peer_select_kernel, nchunk=nchunk),
        grid=(nt // nchunk,),
        in_specs=[pl.BlockSpec((nchunk, scnh.shape[1], LANES), lambda i: (i, 0, 0)), spec],
        out_specs=[spec, spec, spec],
        out_shape=[jax.ShapeDtypeStruct(s2.shape, F32)] * 3,
        compiler_params=_cparams("parallel"),
        name="peer_select",
    )(scnh, s2)


def _gelu(x):
    return 0.5 * x * (1.0 + lax.erf(x * (2.0 ** -0.5)))


def _peer_mix_kernel(xt_ref, u_ref, vt_ref, e2_ref, c1_ref, g1_ref, out_ref, hid_ref, a_ref, *, te, tm, cw):
    nk, nh = PEER_N_KEYS, PEER_HEADS
    nslab = te // nk
    jrows = nk // 2
    tiles = cw // LANES
    s = pl.program_id(1)
    last = pl.num_programs(1) - 1

    def project(c):
        for l in range(tiles):
            lt = c * tiles + l
            for jh in range(2):
                w = [jnp.zeros((jrows, LANES), F32) for _ in range(nslab)]
                for h in range(nh):
                    r0 = h * nk + jh * jrows
                    e2 = e2_ref[lt, r0:r0 + jrows, :]
                    for ii in range(nslab):
                        r = ii * nh + h
                        sel = e2 >= c1_ref[lt, r:r + 1, :]
                        w[ii] = w[ii] + jnp.where(sel, e2 * g1_ref[lt, r:r + 1, :], 0.0)
                for ii in range(nslab):
                    r0 = ii * nk + jh * jrows
                    a = _gelu(hid_ref[lt, r0:r0 + jrows, :]) * w[ii]
                    a_ref[r0:r0 + jrows, lt * LANES:(lt + 1) * LANES] = a.astype(BF16)
        out_ref[:, c * cw:(c + 1) * cw] += _dot(vt_ref[...], a_ref[:, c * cw:(c + 1) * cw])

    def hidden(c):
        hid = _dot(u_ref[...], xt_ref[:, c * cw:(c + 1) * cw])
        for l in range(tiles):
            hid_ref[c * tiles + l] = hid[:, l * LANES:(l + 1) * LANES]

    @pl.when(s == 0)
    def _():
        out_ref[...] = jnp.zeros_like(out_ref)
        for c in range(tm // cw):
            hidden(c)

    @pl.when(jnp.logical_and(s > 0, s < last))
    def _():
        for c in range(tm // cw):
            project(c)
            hidden(c)

    @pl.when(s == last)
    def _():
        for c in range(tm // cw):
            project(c)


def _peer_mix(xt, u_bf, vt_bf, e2, c1, g1):
    d, t = xt.shape
    n_exp = u_bf.shape[0]
    nt, rows, _ = e2.shape
    tm = min(t, TOKEN_TILE)
    te = PEER_EXPERT_TILE
    cw = min(tm, PEER_LANE_CHUNK)
    n = n_exp // te
    crow = te // PEER_N_KEYS * PEER_HEADS
    once = pl.Buffered(1)
    tok = pl.BlockSpec((tm // LANES, rows, LANES), lambda i, s: (i, 0, 0), pipeline_mode=once)
    sel = pl.BlockSpec((tm // LANES, crow, LANES), lambda i, s: (i, jnp.maximum(s - 1, 0), 0))
    return pl.pallas_call(
        functools.partial(_peer_mix_kernel, te=te, tm=tm, cw=cw),
        grid=(t // tm, n + 1),
        in_specs=[pl.BlockSpec((d, tm), lambda i, s: (0, i), pipeline_mode=once),
                  pl.BlockSpec((te, d), lambda i, s: (jnp.minimum(s, n - 1), 0)),
                  pl.BlockSpec((d, te), lambda i, s: (0, jnp.maximum(s - 1, 0))),
                  tok, sel, sel],
        out_specs=pl.BlockSpec((d, tm), lambda i, s: (0, i)),
        out_shape=jax.ShapeDtypeStruct((d, t), F32),
        scratch_shapes=[pltpu.VMEM((tm // LANES, te, LANES), F32), pltpu.VMEM((te, tm), BF16)],
        compiler_params=_cparams("parallel", "arbitrary"),
        name="peer_mix",
    )(xt, u_bf, vt_bf, e2, c1, g1)


def _ple_kernel(h_ref, pt_ref, g_ref, wg_ref, p_ref, wp_ref, fn_ref, out_ref, *, final):
    rc = ROW_CHUNK
    for r0 in range(0, h_ref.shape[0], rc):
        h = h_ref[r0:r0 + rc, :] + pt_ref[:, r0:r0 + rc].T
        hn = _rms(h, g_ref[...]).astype(BF16)
        gate = jax.nn.sigmoid(_dot(hn, wg_ref[...]))
        emb = _dot(p_ref[r0:r0 + rc, :].astype(BF16), wp_ref[...])
        y = h + emb * gate
        if final:
            y = _rms(y, fn_ref[...])
        out_ref[r0:r0 + rc, :] = y


def _ple(h2, peer_t, g, wg_bf, p2, wp_bf, fn, final):
    t, d = h2.shape
    tm = min(t, DENSE_TILE)
    rd = pl.BlockSpec((tm, d), lambda i: (i, 0))
    one = pl.BlockSpec((1, d), lambda i: (0, 0))
    full = lambda a: pl.BlockSpec(a.shape, lambda i: (0, 0))
    return pl.pallas_call(
        functools.partial(_ple_kernel, final=final),
        grid=(t // tm,),
        in_specs=[rd, pl.BlockSpec((d, tm), lambda i: (0, i)), one, full(wg_bf),
                  pl.BlockSpec((tm, p2.shape[1]), lambda i: (i, 0)), full(wp_bf), one],
        out_specs=rd,
        out_shape=jax.ShapeDtypeStruct((t, d), F32),
        compiler_params=_cparams("parallel"),
        name="ple_final" if final else "ple",
    )(h2, peer_t, g.reshape(1, d), wg_bf, p2, wp_bf, fn.reshape(1, d))


def kernel(x, p, positions, attn_norm, w_in, out_norm_a, out_norm_b, w_out, ffn_norm, peer_wq, peer_subkeys,
           peer_u, peer_v, ple_norm, ple_gate, ple_proj, final_norm):
    b, s, d = x.shape
    depth = w_in.shape[0]
    t = b * s
    assert s % (LANES * max(dil for _, dil in DILATED_PATTERNS)) == 0
    assert all(win // dil == HEAD_DIM for win, dil in DILATED_PATTERNS)
    cos, sin = _rope_tables(positions)
    h = x.reshape(t, d)
    for i in range(depth):
        pa, pb = _in_proj(h, attn_norm[i], w_in[i].astype(BF16), cos, sin)
        oa = _dilated_attn(pa, b, s)
        ob = _sb_attn(pb, b, s)
        h = _mix_out(oa, ob, out_norm_a[i], out_norm_b[i], w_out[i].astype(BF16), h)
        xt, scnh, s2 = _peer_scores(h, ffn_norm[i], peer_wq[i].astype(BF16), peer_subkeys[i].astype(BF16))
        c1, g1, e2 = _peer_select(scnh, s2)
        peer_t = _peer_mix(xt, peer_u[i].astype(BF16), peer_v[i].T.astype(BF16), e2, c1, g1)
        h = _ple(h, peer_t, ple_norm[i], ple_gate[i].astype(BF16), p[i].reshape(t, -1),
                 ple_proj[i].astype(BF16), final_norm, final=(i == depth - 1))
    return h.reshape(b, s, d)
```
